```python
import math
import jax, jax.numpy as jnp
from jax import lax
import numpy as np

D_MODEL = 1024
BATCH = 2
SEQ = 8192
DEPTH = 2
DEC_BATCH = 4
DEC_SEQ = 8192
PAST_LEN = 128

N_EVEN = (DEPTH + 1) // 2
N_ODD = DEPTH // 2

FNET_GROUPS = 4
FNET_GROUP_DIM = 128
FNET_WIDTH = FNET_GROUPS * FNET_GROUP_DIM
MLSTM_HEADS = 4
MLSTM_HEAD_DIM = 128
MLSTM_WIDTH = MLSTM_HEADS * MLSTM_HEAD_DIM
MLSTM_CHUNK = 128
MIX_A_WIDTH = FNET_WIDTH + MLSTM_WIDTH
IN_A_WIDTH = FNET_WIDTH + 4 * MLSTM_WIDTH + 4 * MLSTM_HEADS
SSD_GROUPS = 4
SSD_HEADS_PER_GROUP = 4
SSD_HEADS = SSD_GROUPS * SSD_HEADS_PER_GROUP
SSD_HEAD_DIM = 64
SSD_WIDTH = SSD_HEADS * SSD_HEAD_DIM
SSD_STATE = 128
SSD_CONV = 5
SSD_CHUNK = 128
SSD_CONV_CH = SSD_WIDTH + 2 * SSD_GROUPS * SSD_STATE
CONF_WIDTH = 512
CONF_KERNEL = 31
MIX_C_WIDTH = SSD_WIDTH + CONF_WIDTH
IN_C_WIDTH = SSD_WIDTH + SSD_CONV_CH + 2 * SSD_HEADS + 2 * CONF_WIDTH
FFN_HIDDEN = 2048
FFN_CONV = 3
EPS = 1e-6

kernel_name = 'hybrid_fnet_mlstm_ssd_conformer_encoder'

F32 = jnp.float32


def rmsnorm(x, g):
    xf = x.astype(F32)
    y = xf * lax.rsqrt(jnp.mean(jnp.square(xf), axis=-1, keepdims=True) + EPS)
    return (y * g.astype(F32)).astype(x.dtype)


def layernorm(x, g, b):
    xf = x.astype(F32)
    mu = jnp.mean(xf, axis=-1, keepdims=True)
    var = jnp.mean(jnp.square(xf - mu), axis=-1, keepdims=True)
    return ((xf - mu) * lax.rsqrt(var + EPS) * g.astype(F32) + b.astype(F32)).astype(x.dtype)


def depthwise_conv(x, w, b):
    k = w.shape[0]
    pad = (k - 1) // 2
    y = lax.conv_general_dilated(
        x, w[:, None, :].astype(x.dtype), window_strides=(1,), padding=[(pad, k - 1 - pad)],
        dimension_numbers=('NWC', 'WIO', 'NWC'), feature_group_count=x.shape[-1])
    return y + b.astype(x.dtype)


def split_last(h, sizes):
    out, off = [], 0
    for s in sizes:
        out.append(h[..., off:off + s])
        off += s
    return out


def rev(t):
    return jnp.flip(t, axis=1)


def fnet_mix(u):
    bsz, s, _ = u.shape
    ug = u.astype(F32).reshape(bsz, s, FNET_GROUPS, FNET_GROUP_DIM)
    f = jnp.fft.fft2(ug, axes=(1, 3), norm='ortho')
    return jnp.real(f).reshape(bsz, s, FNET_WIDTH)


def mlstm_one_direction(q, k, v, i_pre, f_pre):
    bsz, s, nh, dh = q.shape
    L = MLSTM_CHUNK
    nc = s // L
    q = q.reshape(bsz, nc, L, nh, dh) * (dh ** -0.5)
    k = k.reshape(bsz, nc, L, nh, dh)
    v = v.reshape(bsz, nc, L, nh, dh)
    li = i_pre.reshape(bsz, nc, L, nh)
    b = jnp.cumsum(jax.nn.log_sigmoid(f_pre).reshape(bsz, nc, L, nh), axis=2)
    b_last = b[:, :, -1]
    a = b_last[:, :, None] - b + li

    def step(carry, inp):
        c_st, n_st, m_st = carry
        k_c, v_c, a_c, bl_c = inp
        m_new = jnp.maximum(bl_c + m_st, jnp.max(a_c, axis=1))
        kw = k_c * jnp.exp(a_c - m_new[:, None])[..., None]
        decay = jnp.exp(bl_c + m_st - m_new)
        c_new = decay[..., None, None] * c_st + jnp.einsum('blhk,blhv->bhkv', kw, v_c)
        n_new = decay[..., None] * n_st + jnp.sum(kw, axis=1)
        return (c_new, n_new, m_new), (c_st, n_st, m_st)

    init = (jnp.zeros((bsz, nh, dh, dh), F32), jnp.zeros((bsz, nh, dh), F32), jnp.zeros((bsz, nh), F32))
    xs = (jnp.moveaxis(k, 1, 0), jnp.moveaxis(v, 1, 0), jnp.moveaxis(a, 1, 0), jnp.moveaxis(b_last, 1, 0))
    _, (c_prev, n_prev, m_prev) = lax.scan(step, init, xs)

    tri = jnp.tril(jnp.ones((L, L), dtype=bool))
    d_log = b[:, :, :, None, :] - b[:, :, None, :, :] + li[:, :, None, :, :]
    d_log = jnp.where(tri[None, None, :, :, None], d_log, -jnp.inf)
    inter_log = b + jnp.moveaxis(m_prev, 0, 1)[:, :, None, :]
    m_t = jnp.maximum(inter_log, jnp.max(d_log, axis=3))
    scores = jnp.einsum('bcthd,bcshd->bctsh', q, k) * jnp.exp(d_log - m_t[:, :, :, None, :])
    inter_scale = jnp.exp(inter_log - m_t)
    num = (jnp.einsum('bctsh,bcshd->bcthd', scores, v)
           + inter_scale[..., None] * jnp.einsum('bcthk,cbhkv->bcthv', q, c_prev))
    nq = jnp.sum(scores, axis=3) + inter_scale * jnp.einsum('bcthk,cbhk->bcth', q, n_prev)
    denom = jnp.maximum(jnp.abs(nq), jnp.exp(-m_t))
    return (num / denom[..., None]).reshape(bsz, s, nh, dh)


def mlstm_mix(q, k, v, o, gate_pre, gate_bias):
    bsz, s, _ = q.shape
    shp = (bsz, s, MLSTM_HEADS, MLSTM_HEAD_DIM)
    qf = q.astype(F32).reshape(shp)
    kf = k.astype(F32).reshape(shp)
    vf = v.astype(F32).reshape(shp)
    g = gate_pre.astype(F32).reshape(bsz, s, 2, 2, MLSTM_HEADS) + gate_bias.astype(F32)
    h_fwd = mlstm_one_direction(qf, kf, vf, g[:, :, 0, 0], g[:, :, 1, 0])
    h_bwd = rev(mlstm_one_direction(rev(qf), rev(kf), rev(vf), rev(g[:, :, 0, 1]), rev(g[:, :, 1, 1])))
    h = (h_fwd + h_bwd).reshape(bsz, s, MLSTM_WIDTH)
    return jax.nn.sigmoid(o.astype(F32)) * h


def ssd_one_direction(x, dt, a_neg, bm, cm):
    bsz, s = x.shape[:2]
    L = SSD_CHUNK
    nc = s // L
    G, J, P, N = SSD_GROUPS, SSD_HEADS_PER_GROUP, SSD_HEAD_DIM, SSD_STATE
    x = x.reshape(bsz, nc, L, G, J, P)
    dt = dt.reshape(bsz, nc, L, G, J)
    bm = bm.reshape(bsz, nc, L, G, N)
    cm = cm.reshape(bsz, nc, L, G, N)
    a_cs = jnp.cumsum(dt * a_neg, axis=2)
    a_last = a_cs[:, :, -1]
    tri = jnp.tril(jnp.ones((L, L), dtype=bool))
    seg = a_cs[:, :, :, None] - a_cs[:, :, None]
    decay_ts = jnp.exp(jnp.where(tri[None, None, :, :, None, None], seg, -jnp.inf))
    cb = jnp.einsum('bctgn,bcsgn->bctsg', cm, bm)
    w = cb[..., None] * decay_ts * dt[:, :, None]
    y_diag = jnp.einsum('bctsgj,bcsgjp->bctgjp', w, x)
    xw = x * (jnp.exp(a_last[:, :, None] - a_cs) * dt)[..., None]
    states = jnp.einsum('bclgn,bclgjp->bcgjpn', bm, xw)

    def step(h, inp):
        st, al = inp
        return jnp.exp(al)[..., None, None] * h + st, h

    h0 = jnp.zeros((bsz, G, J, P, N), F32)
    _, h_prev = lax.scan(step, h0, (jnp.moveaxis(states, 1, 0), jnp.moveaxis(a_last, 1, 0)))
    y_off = jnp.einsum('bctgn,cbgjpn->bctgjp', cm, h_prev) * jnp.exp(a_cs)[..., None]
    return (y_diag + y_off).reshape(bsz, s, G, J, P)


def ssd_mix(z, xbc, dt_pre, conv_w, conv_b, dt_bias, a_log, d_skip, norm_g):
    bsz, s, _ = z.shape
    G, J, P, N = SSD_GROUPS, SSD_HEADS_PER_GROUP, SSD_HEAD_DIM, SSD_STATE
    xbc = jax.nn.silu(depthwise_conv(xbc, conv_w, conv_b)).astype(F32)
    xs, bm, cm = split_last(xbc, [SSD_WIDTH, G * N, G * N])
    xs = xs.reshape(bsz, s, G, J, P)
    bm = bm.reshape(bsz, s, G, N)
    cm = cm.reshape(bsz, s, G, N)
    dt = jax.nn.softplus(dt_pre.astype(F32).reshape(bsz, s, 2, G, J) + dt_bias.astype(F32))
    a_neg = -jnp.exp(a_log.astype(F32))
    y_fwd = ssd_one_direction(xs, dt[:, :, 0], a_neg[0], bm, cm)
    y_bwd = rev(ssd_one_direction(rev(xs), rev(dt[:, :, 1]), a_neg[1], rev(bm), rev(cm)))
    y = y_fwd + y_bwd + d_skip.astype(F32)[..., None] * xs
    y = y.reshape(bsz, s, SSD_WIDTH) * jax.nn.silu(z.astype(F32))
    return rmsnorm(y, norm_g)


def conformer_conv_mix(val, gate, conv_w, conv_b, ln_g, ln_b):
    u = val * jax.nn.sigmoid(gate)
    u = depthwise_conv(u, conv_w, conv_b)
    return jax.nn.silu(layernorm(u, ln_g, ln_b))


def conv_ffn(x, norm_g, w_up, conv_w, conv_b, w_down):
    h = rmsnorm(x, norm_g) @ w_up
    h = depthwise_conv(h, conv_w, conv_b)
    g, v = split_last(h, [FFN_HIDDEN, FFN_HIDDEN])
    return (jax.nn.silu(g) * v) @ w_down


def layer_a(x, norm_g, w_in, gate_bias, w_out):
    h = rmsnorm(x, norm_g) @ w_in
    u_f, q, k, v, o, g = split_last(h, [FNET_WIDTH, MLSTM_WIDTH, MLSTM_WIDTH, MLSTM_WIDTH, MLSTM_WIDTH, 4 * MLSTM_HEADS])
    mix = jnp.concatenate([fnet_mix(u_f).astype(x.dtype),
                           mlstm_mix(q, k, v, o, g, gate_bias).astype(x.dtype)], axis=-1)
    return x + mix @ w_out


def layer_c(x, norm_g, w_in, ssd_conv_w, ssd_conv_b, dt_bias, a_log, d_skip, ssd_norm,
            conf_conv_w, conf_conv_b, conf_ln_g, conf_ln_b, w_out):
    h = rmsnorm(x, norm_g) @ w_in
    z, xbc, dt_pre, c_val, c_gate = split_last(h, [SSD_WIDTH, SSD_CONV_CH, 2 * SSD_HEADS, CONF_WIDTH, CONF_WIDTH])
    y_ssd = ssd_mix(z, xbc, dt_pre, ssd_conv_w, ssd_conv_b, dt_bias, a_log, d_skip, ssd_norm)
    y_conf = conformer_conv_mix(c_val, c_gate, conf_conv_w, conf_conv_b, conf_ln_g, conf_ln_b)
    mix = jnp.concatenate([y_ssd.astype(x.dtype), y_conf.astype(x.dtype)], axis=-1)
    return x + mix @ w_out


def encoder(x, a_norm, a_w_in, a_gate_bias, a_w_out,
            c_norm, c_w_in, c_ssd_conv_w, c_ssd_conv_b, c_dt_bias, c_a_log, c_d_skip, c_ssd_norm,
            c_conf_conv_w, c_conf_conv_b, c_conf_ln_g, c_conf_ln_b, c_w_out,
            ffn_norm, ffn_w_up, ffn_conv_w, ffn_conv_b, ffn_w_down, final_norm):
    for layer in range(DEPTH):
        j = layer // 2
        if layer % 2 == 0:
            x = layer_a(x, a_norm[j], a_w_in[j], a_gate_bias[j], a_w_out[j])
        else:
            x = layer_c(x, c_norm[j], c_w_in[j], c_ssd_conv_w[j], c_ssd_conv_b[j], c_dt_bias[j], c_a_log[j],
                        c_d_skip[j], c_ssd_norm[j], c_conf_conv_w[j], c_conf_conv_b[j], c_conf_ln_g[j],
                        c_conf_ln_b[j], c_w_out[j])
        x = x + conv_ffn(x, ffn_norm[layer], ffn_w_up[layer], ffn_conv_w[layer], ffn_conv_b[layer], ffn_w_down[layer])
    return rmsnorm(x, final_norm)


def setup_inputs(seed: int = 0) -> dict:
    key = jax.random.key(seed)
    ks = jax.random.split(key, 32)

    def nrm(k, shape, scale):
        return jax.random.normal(k, shape, F32) * scale

    def gain(k, shape):
        return 1.0 + 0.01 * jax.random.normal(k, shape, F32)

    H = MLSTM_HEADS
    i_bias = nrm(ks[4], (N_EVEN, 1, 2, H), 0.1)
    f_bias = jnp.linspace(3.0, 6.0, H, dtype=F32)[None, None, None, :] + nrm(ks[5], (N_EVEN, 1, 2, H), 0.1)
    gate_bias = jnp.concatenate([i_bias, f_bias], axis=1)
    dt0 = jnp.exp(jax.random.uniform(ks[10], (N_ODD, 2, SSD_GROUPS, SSD_HEADS_PER_GROUP), F32,
                                     minval=math.log(1e-3), maxval=math.log(1e-1)))
    dt_bias = dt0 + jnp.log(-jnp.expm1(-dt0))
    a_log = jnp.log(jax.random.uniform(ks[11], (N_ODD, 2, SSD_GROUPS, SSD_HEADS_PER_GROUP), F32,
                                       minval=1.0, maxval=16.0))
    return {
        'x_prompt': nrm(ks[0], (BATCH, SEQ, D_MODEL), 1.0),
        'x_sample': nrm(ks[1], (DEC_BATCH, DEC_SEQ, D_MODEL), 1.0),
        'a_norm': gain(ks[2], (N_EVEN, D_MODEL)),
        'a_w_in': nrm(ks[3], (N_EVEN, D_MODEL, IN_A_WIDTH), D_MODEL ** -0.5),
        'a_gate_bias': gate_bias,
        'a_w_out': nrm(ks[6], (N_EVEN, MIX_A_WIDTH, D_MODEL), MIX_A_WIDTH ** -0.5),
        'c_norm': gain(ks[7], (N_ODD, D_MODEL)),
        'c_w_in': nrm(ks[8], (N_ODD, D_MODEL, IN_C_WIDTH), D_MODEL ** -0.5),
        'c_ssd_conv_w': nrm(ks[9], (N_ODD, SSD_CONV, SSD_CONV_CH), SSD_CONV ** -0.5),
        'c_ssd_conv_b': nrm(ks[12], (N_ODD, SSD_CONV_CH), 0.01),
        'c_dt_bias': dt_bias,
        'c_a_log': a_log,
        'c_d_skip': 1.0 + nrm(ks[13], (N_ODD, SSD_GROUPS, SSD_HEADS_PER_GROUP), 0.1),
        'c_ssd_norm': gain(ks[14], (N_ODD, SSD_WIDTH)),
        'c_conf_conv_w': nrm(ks[15], (N_ODD, CONF_KERNEL, CONF_WIDTH), CONF_KERNEL ** -0.5),
        'c_conf_conv_b': nrm(ks[16], (N_ODD, CONF_WIDTH), 0.01),
        'c_conf_ln_g': gain(ks[17], (N_ODD, CONF_WIDTH)),
        'c_conf_ln_b': nrm(ks[18], (N_ODD, CONF_WIDTH), 0.01),
        'c_w_out': nrm(ks[19], (N_ODD, MIX_C_WIDTH, D_MODEL), MIX_C_WIDTH ** -0.5),
        'ffn_norm': gain(ks[20], (DEPTH, D_MODEL)),
        'ffn_w_up': nrm(ks[21], (DEPTH, D_MODEL, 2 * FFN_HIDDEN), D_MODEL ** -0.5),
        'ffn_conv_w': nrm(ks[22], (DEPTH, FFN_CONV, 2 * FFN_HIDDEN), FFN_CONV ** -0.5),
        'ffn_conv_b': nrm(ks[23], (DEPTH, 2 * FFN_HIDDEN), 0.01),
        'ffn_w_down': nrm(ks[24], (DEPTH, FFN_HIDDEN, D_MODEL), FFN_HIDDEN ** -0.5),
        'final_norm': gain(ks[25], (D_MODEL,)),
    }


def reference(x_prompt, x_sample, a_norm, a_w_in, a_gate_bias, a_w_out,
              c_norm, c_w_in, c_ssd_conv_w, c_ssd_conv_b, c_dt_bias, c_a_log, c_d_skip, c_ssd_norm,
              c_conf_conv_w, c_conf_conv_b, c_conf_ln_g, c_conf_ln_b, c_w_out,
              ffn_norm, ffn_w_up, ffn_conv_w, ffn_conv_b, ffn_w_down, final_norm):
    y_prompt = encoder(x_prompt, a_norm, a_w_in, a_gate_bias, a_w_out,
                       c_norm, c_w_in, c_ssd_conv_w, c_ssd_conv_b, c_dt_bias, c_a_log, c_d_skip, c_ssd_norm,
                       c_conf_conv_w, c_conf_conv_b, c_conf_ln_g, c_conf_ln_b, c_w_out,
                       ffn_norm, ffn_w_up, ffn_conv_w, ffn_conv_b, ffn_w_down, final_norm)
    y_sample = encoder(x_sample, a_norm, a_w_in, a_gate_bias, a_w_out,
                       c_norm, c_w_in, c_ssd_conv_w, c_ssd_conv_b, c_dt_bias, c_a_log, c_d_skip, c_ssd_norm,
                       c_conf_conv_w, c_conf_conv_b, c_conf_ln_g, c_conf_ln_b, c_w_out,
                       ffn_norm, ffn_w_up, ffn_conv_w, ffn_conv_b, ffn_w_down, final_norm)
    return (y_prompt, y_sample)
```

```python
import functools
import math

import numpy as np
import jax
import jax.numpy as jnp
from jax import lax
from jax.experimental import pallas as pl
from jax.experimental.pallas import tpu as pltpu

F32 = jnp.float32
BF16 = jnp.bfloat16
EPS = 1e-6

LANES = 128
SUBLANES = 8
CHUNK = 128
VMEM_LIMIT = 56 * 1024 * 1024

FNET_GROUPS = 4
FNET_DIM = 128
MLSTM_HEADS = 4
MLSTM_DIM = 128
SSD_GROUPS = 4
SSD_HPG = 4
SSD_P = 64
SSD_N = 128
CONF_W = 512

TM_PROJ = 512
TM_FFN = 512
TM_CONV = 512
HALO = 8
HALO_CONF = 16
FFN_HC = 512

_NT = (((1,), (1,)), ((), ()))
_TN = (((0,), (0,)), ((), ()))


def _cparams(*sem):
    return pltpu.CompilerParams(dimension_semantics=sem, vmem_limit_bytes=VMEM_LIMIT)


def _const_spec(shape):
    nd = len(shape)
    return pl.BlockSpec(shape, lambda *_: (0,) * nd, pipeline_mode=pl.Buffered(1))


def _dot(a, b):
    return jnp.dot(a, b, preferred_element_type=F32)


def _rms(x, g):
    return x * lax.rsqrt(jnp.mean(x * x, axis=-1, keepdims=True) + EPS) * g


def _softplus(x):
    return jnp.maximum(x, 0.0) + jnp.log1p(jnp.exp(-jnp.abs(x)))


def _log_sigmoid(x):
    return -_softplus(-x)


def _silu(x):
    return x * jax.nn.sigmoid(x)


def _chunk_scans(v, fwd_rows):
    lane = lax.broadcasted_iota(jnp.int32, v.shape, 1)
    pre, suf = v, v
    k = 1
    while k < CHUNK:
        pre = pre + jnp.where(lane >= k, pltpu.roll(pre, k, axis=1), 0.0)
        suf = suf + jnp.where(lane < CHUNK - k, pltpu.roll(suf, CHUNK - k, axis=1), 0.0)
        k *= 2
    return jnp.where(fwd_rows, pre, suf)


def _rows_to_cols(rows):
    r = rows.shape[0]
    if r < CHUNK:
        rows = jnp.concatenate([rows, jnp.zeros((CHUNK - r, CHUNK), F32)], axis=0)
    return rows.T


def _a_in_kernel(x_ref, g_ref, wf_ref, dft_ref, wq_ref, wgt_ref, vr_ref, vi_ref, qkvo_ref, gt_ref):
    xb = _rms(x_ref[0], g_ref[...]).astype(BF16)
    hf = _dot(xb, wf_ref[...]).astype(BF16)
    for g in range(FNET_GROUPS):
        sl = slice(g * FNET_DIM, (g + 1) * FNET_DIM)
        pq = _dot(hf[:, sl], dft_ref[...])
        vr_ref[0, :, sl] = pq[:, :FNET_DIM].astype(BF16)
        vi_ref[0, :, sl] = pq[:, FNET_DIM:].astype(BF16)
    nq = qkvo_ref.shape[2]
    for j in range(nq // 512):
        sl = slice(j * 512, (j + 1) * 512)
        qkvo_ref[0, :, sl] = _dot(xb, wq_ref[:, sl]).astype(BF16)
    gt_ref[0] = lax.dot_general(wgt_ref[...], xb, _NT, preferred_element_type=F32)


def _a_in(x, g, wf, dft, wq, wgt):
    b, s, d = x.shape
    tm = min(TM_PROJ, s)
    fw = wf.shape[1]
    nq = wq.shape[1]
    ng = wgt.shape[0]
    return pl.pallas_call(
        _a_in_kernel,
        grid=(b, s // tm),
        in_specs=[
            pl.BlockSpec((1, tm, d), lambda i, m: (i, m, 0)),
            _const_spec((1, d)), _const_spec(wf.shape), _const_spec(dft.shape),
            _const_spec(wq.shape), _const_spec(wgt.shape),
        ],
        out_specs=[
            pl.BlockSpec((1, tm, fw), lambda i, m: (i, m, 0)),
            pl.BlockSpec((1, tm, fw), lambda i, m: (i, m, 0)),
            pl.BlockSpec((1, tm, nq), lambda i, m: (i, m, 0)),
            pl.BlockSpec((1, ng, tm), lambda i, m: (i, 0, m)),
        ],
        out_shape=[
            jax.ShapeDtypeStruct((b, s, fw), BF16),
            jax.ShapeDtypeStruct((b, s, fw), BF16),
            jax.ShapeDtypeStruct((b, s, nq), BF16),
            jax.ShapeDtypeStruct((b, ng, s), F32),
        ],
        compiler_params=_cparams("parallel", "parallel"),
        name="a_in",
    )(x, g, wf, dft, wq, wgt)


def _fft1_kernel(vr_ref, vi_ref, w1_ref, tc_ref, ts_ref, yr_ref, yi_ref):
    s1 = vr_ref.shape[1]
    ts2 = tc_ref.shape[0]
    cw = yr_ref.shape[3]
    v = jnp.concatenate([vr_ref[0], vi_ref[0]], axis=0)
    y = _dot(w1_ref[...], v)
    for i in range(ts2):
        tc = tc_ref[i]
        ts = ts_ref[i]
        for g in range(cw // LANES):
            sl = slice(i * cw + g * LANES, i * cw + (g + 1) * LANES)
            osl = slice(g * LANES, (g + 1) * LANES)
            yr = y[:s1, sl]
            yi = y[s1:, sl]
            yr_ref[0, i, :, osl] = (yr * tc + yi * ts).astype(BF16)
            yi_ref[0, i, :, osl] = (yi * tc - yr * ts).astype(BF16)


def _fft1(vr, vi, w1, tc, ts):
    b, s, cw = vr.shape
    s1 = s // LANES
    ts2 = 8
    vr2 = vr.reshape(b, s1, LANES * cw)
    vi2 = vi.reshape(b, s1, LANES * cw)
    spec_in = pl.BlockSpec((1, s1, ts2 * cw), lambda j, i: (i, 0, j))
    spec_tw = pl.BlockSpec((ts2, s1, LANES), lambda j, i: (j, 0, 0))
    spec_out = pl.BlockSpec((1, ts2, s1, cw), lambda j, i: (i, j, 0, 0))
    return pl.pallas_call(
        _fft1_kernel,
        grid=(LANES // ts2, b),
        in_specs=[spec_in, spec_in, _const_spec(w1.shape), spec_tw, spec_tw],
        out_specs=[spec_out, spec_out],
        out_shape=[jax.ShapeDtypeStruct((b, LANES, s1, cw), BF16)] * 2,
        compiler_params=_cparams("parallel", "parallel"),
        name="fft1",
    )(vr2, vi2, w1, tc, ts)


def _fft2_kernel(yr_ref, yi_ref, w2_ref, o_ref, *, scale):
    y = jnp.concatenate([yr_ref[0], yi_ref[0]], axis=0)
    o_ref[0] = (_dot(w2_ref[...], y) * scale).astype(BF16)


def _fft2(yr, yi, w2, scale):
    b, _, s1, cw = yr.shape
    n = s1 * cw
    tn = min(4096, n)
    yr2 = yr.reshape(b, LANES, n)
    yi2 = yi.reshape(b, LANES, n)
    spec = pl.BlockSpec((1, LANES, tn), lambda i, j: (i, 0, j))
    out = pl.pallas_call(
        functools.partial(_fft2_kernel, scale=scale),
        grid=(b, n // tn),
        in_specs=[spec, spec, _const_spec(w2.shape)],
        out_specs=spec,
        out_shape=jax.ShapeDtypeStruct((b, LANES, n), BF16),
        compiler_params=_cparams("parallel", "parallel"),
        name="fft2",
    )(yr2, yi2, w2)
    return out.reshape(b, LANES * s1, cw)


def _mlstm_kernel(qf_ref, kf_ref, vf_ref, qb_ref, kb_ref, vb_ref, gtf_ref, gtb_ref, bias_ref,
                  hf_ref, hb_ref, cst_ref, mst_ref):
    nh, dh = MLSTM_HEADS, MLSTM_DIM
    nu = 2 * nh
    L = CHUNK

    @pl.when(pl.program_id(1) == 0)
    def _():
        cst_ref[...] = jnp.zeros_like(cst_ref)
        mst_ref[...] = jnp.zeros_like(mst_ref)

    isf = lax.broadcasted_iota(jnp.int32, (nu, L), 0) < nh
    gtf = gtf_ref[0]
    gtb = gtb_ref[0]
    bias = bias_ref[...]
    li = jnp.where(isf, gtf[0:nu], gtb[0:nu]) + bias[0:nu]
    lf = _log_sigmoid(jnp.where(isf, gtf[nu:2 * nu], gtb[nu:2 * nu]) + bias[nu:2 * nu])
    bcs = _chunk_scans(lf, isf)
    b_last = jnp.sum(lf, axis=1, keepdims=True)
    m_prev = mst_ref[:, 0:1]
    a = b_last - bcs + li
    m_new = jnp.maximum(b_last + m_prev, jnp.max(a, axis=1, keepdims=True))
    ea = jnp.exp(a - m_new)
    decay = jnp.exp(b_last + m_prev - m_new)
    e = li - bcs
    b_cols = _rows_to_cols(bcs)

    t_i = lax.broadcasted_iota(jnp.int32, (L, L), 0)
    s_i = lax.broadcasted_iota(jnp.int32, (L, L), 1)
    ones = jnp.ones((L, dh), BF16)
    scale = dh ** -0.5
    for d in range(2):
        q_ref, k_ref, v_ref, o_ref = (qf_ref, kf_ref, vf_ref, hf_ref) if d == 0 else (qb_ref, kb_ref, vb_ref, hb_ref)
        tri = (s_i <= t_i) if d == 0 else (s_i >= t_i)
        for h in range(nh):
            u = d * nh + h
            q = q_ref[0, :, h * dh:(h + 1) * dh]
            k = k_ref[0, :, h * dh:(h + 1) * dh]
            v = v_ref[0, :, h * dh:(h + 1) * dh]
            vaug = jnp.concatenate([v, ones], axis=1)
            b_c = b_cols[:, u:u + 1]
            dlog = jnp.where(tri, b_c + e[u:u + 1], -jnp.inf)
            ilog = b_c + m_prev[u:u + 1]
            m_t = jnp.maximum(ilog, jnp.max(dlog, axis=1, keepdims=True))
            w = jnp.exp(dlog - m_t)
            sc = lax.dot_general(q, k, _NT, preferred_element_type=F32) * scale
            cs = cst_ref[u]
            tot = _dot((sc * w).astype(BF16), vaug) + jnp.exp(ilog - m_t) * (_dot(q, cs.astype(BF16)) * scale)
            den = jnp.maximum(jnp.abs(tot[:, dh:]), jnp.exp(-m_t))
            o_ref[0, :, h * dh:(h + 1) * dh] = tot[:, :dh] / den
            kw = (k.T.astype(F32) * ea[u:u + 1]).astype(BF16)
            cst_ref[u] = decay[u:u + 1] * cs + _dot(kw, vaug)
    mst_ref[...] = jnp.broadcast_to(m_new, mst_ref.shape)


def _mlstm(qkvo, gt, bias_col):
    b, s, _ = qkvo.shape
    nc = s // CHUNK
    w = MLSTM_HEADS * MLSTM_DIM
    ng = gt.shape[1]

    def col(j, rev):
        if rev:
            return pl.BlockSpec((1, CHUNK, w), lambda i, c: (i, nc - 1 - c, j))
        return pl.BlockSpec((1, CHUNK, w), lambda i, c: (i, c, j))

    gspec_f = pl.BlockSpec((1, ng, CHUNK), lambda i, c: (i, 0, c))
    gspec_b = pl.BlockSpec((1, ng, CHUNK), lambda i, c: (i, 0, nc - 1 - c))
    return pl.pallas_call(
        _mlstm_kernel,
        grid=(b, nc),
        in_specs=[col(0, False), col(1, False), col(2, False), col(0, True), col(1, True), col(2, True),
                  gspec_f, gspec_b, _const_spec(bias_col.shape)],
        out_specs=[col(0, False), col(0, True)],
        out_shape=[jax.ShapeDtypeStruct((b, s, w), F32)] * 2,
        scratch_shapes=[pltpu.VMEM((2 * MLSTM_HEADS, MLSTM_DIM, 2 * MLSTM_DIM), F32),
                        pltpu.VMEM((2 * MLSTM_HEADS, LANES), F32)],
        compiler_params=_cparams("parallel", "arbitrary"),
        name="mlstm",
    )(qkvo, qkvo, qkvo, qkvo, qkvo, qkvo, gt, gt, bias_col)


def _a_out_kernel(x_ref, fn_ref, hf_ref, hb_ref, o_ref, w_ref, y_ref):
    fw = fn_ref.shape[2]
    mix = (jax.nn.sigmoid(o_ref[0].astype(F32)) * (hf_ref[0] + hb_ref[0])).astype(BF16)
    y_ref[0] = x_ref[0] + _dot(fn_ref[0], w_ref[0:fw]) + _dot(mix, w_ref[fw:])


def _a_out(x, fn, hf, hb, qkvo, w):
    b, s, d = x.shape
    tm = min(TM_PROJ, s)
    fw = fn.shape[2]
    mw = hf.shape[2]
    tok = lambda width, j=0: pl.BlockSpec((1, tm, width), lambda i, m: (i, m, j))
    return pl.pallas_call(
        _a_out_kernel,
        grid=(b, s // tm),
        in_specs=[tok(d), tok(fw), tok(mw), tok(mw), tok(mw, 3), _const_spec(w.shape)],
        out_specs=tok(d),
        out_shape=jax.ShapeDtypeStruct((b, s, d), F32),
        compiler_params=_cparams("parallel", "parallel"),
        name="a_out",
    )(x, fn, hf, hb, qkvo, w)


def _halo_specs(tm, halo, width, s):
    r = tm // halo
    nb = s // halo
    mid = pl.BlockSpec((1, tm, width), lambda i, m: (i, m, 0))
    top = pl.BlockSpec((1, halo, width), lambda i, m: (i, jnp.maximum(m * r - 1, 0), 0))
    bot = pl.BlockSpec((1, halo, width), lambda i, m: (i, jnp.minimum((m + 1) * r, nb - 1), 0))
    return mid, top, bot


def _ffn_kernel(x_ref, top_ref, bot_ref, g_ref, wu_ref, cw_ref, cb_ref, wd_ref, fg_ref, y_ref, a_scr, *, final_norm):
    m = pl.program_id(1)
    tm = x_ref.shape[1]
    fh = wd_ref.shape[0]
    g = g_ref[...]
    x = x_ref[0]
    top = jnp.where(m > 0, _rms(top_ref[0], g), 0.0)
    bot = jnp.where(m < pl.num_programs(1) - 1, _rms(bot_ref[0], g), 0.0)
    xe = jnp.concatenate([top, _rms(x, g), bot], axis=0).astype(BF16)
    n_ext = tm + 2 * HALO

    def conv3(h, c0):
        w = cw_ref[:, c0:c0 + FFN_HC]
        out = (w[0:1] * pltpu.roll(h, 1, axis=0) + w[1:2] * h + w[2:3] * pltpu.roll(h, n_ext - 1, axis=0))
        return out[HALO:HALO + tm] + cb_ref[:, c0:c0 + FFN_HC]

    for j in range(fh // FFN_HC):
        c0 = j * FFN_HC
        gate = conv3(_dot(xe, wu_ref[:, c0:c0 + FFN_HC]), c0)
        val = conv3(_dot(xe, wu_ref[:, fh + c0:fh + c0 + FFN_HC]), fh + c0)
        a_scr[:, c0:c0 + FFN_HC] = (_silu(gate) * val).astype(BF16)
    y = x + _dot(a_scr[...], wd_ref[...])
    if final_norm:
        y = _rms(y, fg_ref[...])
    y_ref[0] = y


def _ffn(x, g, wu, cw, cb, wd, fg, final_norm):
    b, s, d = x.shape
    tm = min(TM_FFN, s)
    fh = wd.shape[0]
    mid, top, bot = _halo_specs(tm, HALO, d, s)
    return pl.pallas_call(
        functools.partial(_ffn_kernel, final_norm=final_norm),
        grid=(b, s // tm),
        in_specs=[mid, top, bot, _const_spec((1, d)), _const_spec(wu.shape), _const_spec(cw.shape),
                  _const_spec(cb.shape), _const_spec(wd.shape), _const_spec((1, d))],
        out_specs=mid,
        out_shape=jax.ShapeDtypeStruct((b, s, d), F32),
        scratch_shapes=[pltpu.VMEM((tm, fh), BF16)],
        compiler_params=_cparams("parallel", "parallel"),
        name="ffn",
    )(x, x, x, g, wu, cw, cb, wd, fg)


def _c_in_kernel(x_ref, g_ref, w_ref, wdt_ref, z_ref, xbc_ref, cvg_ref, dtt_ref):
    xb = _rms(x_ref[0], g_ref[...]).astype(BF16)
    zw = z_ref.shape[2]
    xw = xbc_ref.shape[2]
    cw = cvg_ref.shape[2]
    for j in range(zw // 512):
        sl = slice(j * 512, (j + 1) * 512)
        z_ref[0, :, sl] = _dot(xb, w_ref[:, sl]).astype(BF16)
    for j in range(xw // 512):
        sl = slice(j * 512, (j + 1) * 512)
        xbc_ref[0, :, sl] = _dot(xb, w_ref[:, zw + j * 512:zw + (j + 1) * 512])
    for j in range(cw // 512):
        sl = slice(j * 512, (j + 1) * 512)
        cvg_ref[0, :, sl] = _dot(xb, w_ref[:, zw + xw + j * 512:zw + xw + (j + 1) * 512])
    dtt_ref[0] = lax.dot_general(wdt_ref[...], xb, _NT, preferred_element_type=F32)


def _c_in(x, g, w, wdt, zw, xw, cw):
    b, s, d = x.shape
    tm = min(TM_PROJ, s)
    nd = wdt.shape[0]
    tok = lambda width: pl.BlockSpec((1, tm, width), lambda i, m: (i, m, 0))
    return pl.pallas_call(
        _c_in_kernel,
        grid=(b, s // tm),
        in_specs=[tok(d), _const_spec((1, d)), _const_spec(w.shape), _const_spec(wdt.shape)],
        out_specs=[tok(zw), tok(xw), tok(cw), pl.BlockSpec((1, nd, tm), lambda i, m: (i, 0, m))],
        out_shape=[jax.ShapeDtypeStruct((b, s, zw), BF16), jax.ShapeDtypeStruct((b, s, xw), F32),
                   jax.ShapeDtypeStruct((b, s, cw), F32), jax.ShapeDtypeStruct((b, nd, s), F32)],
        compiler_params=_cparams("parallel", "parallel"),
        name="c_in",
    )(x, g, w, wdt)


def _dwconv_taps(scr, w_ref, tm, halo):
    k = w_ref.shape[0]
    pad = (k - 1) // 2
    acc = w_ref[0:1, :] * scr[pl.ds(halo - pad, tm), :]
    for j in range(1, k):
        acc = acc + w_ref[j:j + 1, :] * scr[pl.ds(halo - pad + j, tm), :]
    return acc


def _ssd_conv_kernel(x_ref, top_ref, bot_ref, w_ref, b_ref, o_ref, scr):
    m = pl.program_id(1)
    tm = x_ref.shape[1]
    scr[0:HALO] = jnp.where(m > 0, top_ref[0], 0.0)
    scr[HALO:HALO + tm] = x_ref[0]
    scr[HALO + tm:] = jnp.where(m < pl.num_programs(1) - 1, bot_ref[0], 0.0)
    o_ref[0] = _silu(_dwconv_taps(scr, w_ref, tm, HALO) + b_ref[...]).astype(o_ref.dtype)


def _ssd_conv(x, w, bias):
    b, s, c = x.shape
    tm = min(TM_CONV, s)
    tc = 512
    r = tm // HALO
    nb = s // HALO
    mid = pl.BlockSpec((1, tm, tc), lambda i, m, j: (i, m, j))
    top = pl.BlockSpec((1, HALO, tc), lambda i, m, j: (i, jnp.maximum(m * r - 1, 0), j))
    bot = pl.BlockSpec((1, HALO, tc), lambda i, m, j: (i, jnp.minimum((m + 1) * r, nb - 1), j))
    return pl.pallas_call(
        _ssd_conv_kernel,
        grid=(b, s // tm, c // tc),
        in_specs=[mid, top, bot, pl.BlockSpec((w.shape[0], tc), lambda i, m, j: (0, j)),
                  pl.BlockSpec((1, tc), lambda i, m, j: (0, j))],
        out_specs=mid,
        out_shape=jax.ShapeDtypeStruct((b, s, c), BF16),
        scratch_shapes=[pltpu.VMEM((tm + 2 * HALO, tc), F32)],
        compiler_params=_cparams("parallel", "parallel", "parallel"),
        name="ssd_conv",
    )(x, x, x, w, bias)


def _conf_kernel(x_ref, top_ref, bot_ref, w_ref, b_ref, lg_ref, lb_ref, o_ref, scr):
    m = pl.program_id(1)
    tm = x_ref.shape[1]
    cw = o_ref.shape[2]

    def glu(t):
        return t[:, :cw] * jax.nn.sigmoid(t[:, cw:])

    scr[0:HALO_CONF] = jnp.where(m > 0, glu(top_ref[0]), 0.0)
    scr[HALO_CONF:HALO_CONF + tm] = glu(x_ref[0])
    scr[HALO_CONF + tm:] = jnp.where(m < pl.num_programs(1) - 1, glu(bot_ref[0]), 0.0)
    u = _dwconv_taps(scr, w_ref, tm, HALO_CONF) + b_ref[...]
    mu = jnp.mean(u, axis=-1, keepdims=True)
    uc = u - mu
    var = jnp.mean(uc * uc, axis=-1, keepdims=True)
    o_ref[0] = _silu(uc * lax.rsqrt(var + EPS) * lg_ref[...] + lb_ref[...]).astype(o_ref.dtype)


def _conf(cvg, w, bias, lg, lb):
    b, s, c2 = cvg.shape
    cw = c2 // 2
    tm = min(256, s)
    mid, top, bot = _halo_specs(tm, HALO_CONF, c2, s)
    return pl.pallas_call(
        _conf_kernel,
        grid=(b, s // tm),
        in_specs=[mid, top, bot, _const_spec(w.shape), _const_spec((1, cw)), _const_spec((1, cw)),
                  _const_spec((1, cw))],
        out_specs=pl.BlockSpec((1, tm, cw), lambda i, m: (i, m, 0)),
        out_shape=jax.ShapeDtypeStruct((b, s, cw), BF16),
        scratch_shapes=[pltpu.VMEM((tm + 2 * HALO_CONF, cw), F32)],
        compiler_params=_cparams("parallel", "parallel"),
        name="conf_conv",
    )(cvg, cvg, cvg, w, bias, lg, lb)


def _ssd_kernel(xf_ref, bf_ref, cf_ref, xb_ref, bb_ref, cb_ref, dtf_ref, dtb_ref, dtbias_ref, aneg_ref,
                yf_ref, yb_ref, hst_ref):
    ng, nj, hp, ns = SSD_GROUPS, SSD_HPG, SSD_P, SSD_N
    nh = ng * nj
    nr = 2 * nh
    gw = nj * hp
    L = CHUNK

    @pl.when(pl.program_id(1) == 0)
    def _():
        hst_ref[...] = jnp.zeros_like(hst_ref)

    isf = lax.broadcasted_iota(jnp.int32, (nr, L), 0) < nh
    dt = _softplus(jnp.where(isf, dtf_ref[0], dtb_ref[0]) + dtbias_ref[...])
    da = dt * aneg_ref[...]
    acs = _chunk_scans(da, isf)
    a_last = jnp.sum(da, axis=1, keepdims=True)
    wdec = jnp.exp(a_last - acs) * dt
    eacs = jnp.exp(acs)
    elast = jnp.exp(a_last)
    cols = _rows_to_cols(jnp.concatenate([acs, wdec, eacs], axis=0))

    t_i = lax.broadcasted_iota(jnp.int32, (L, L), 0)
    s_i = lax.broadcasted_iota(jnp.int32, (L, L), 1)
    hid = lax.broadcasted_iota(jnp.int32, (1, gw), 1) // hp
    hmask = [(hid == j).astype(F32).astype(BF16) for j in range(nj)]

    def per_head(vals):
        out = vals[nj - 1]
        for j in range(nj - 2, -1, -1):
            out = jnp.where(hid == j, vals[j], out)
        return out

    for d in range(2):
        x_ref, b_ref, c_ref, y_ref = (xf_ref, bf_ref, cf_ref, yf_ref) if d == 0 else (xb_ref, bb_ref, cb_ref, yb_ref)
        tri = (s_i <= t_i) if d == 0 else (s_i >= t_i)
        for g in range(ng):
            u = d * ng + g
            rows = [d * nh + g * nj + j for j in range(nj)]
            x = x_ref[0, :, g * gw:(g + 1) * gw]
            bm = b_ref[0, :, g * ns:(g + 1) * ns]
            cm = c_ref[0, :, g * ns:(g + 1) * ns]
            cbm = lax.dot_general(cm, bm, _NT, preferred_element_type=F32)
            yd = None
            for j, r in enumerate(rows):
                seg = cols[:, r:r + 1] - acs[r:r + 1]
                dec = jnp.exp(jnp.where(tri, seg, -jnp.inf))
                wj = (cbm * dec * dt[r:r + 1]).astype(BF16)
                part = _dot(wj, x * hmask[j])
                yd = part if yd is None else yd + part
            hprev = hst_ref[u]
            yoff = _dot(cm, hprev.astype(BF16))
            esc = per_head([cols[:, 2 * nr + r:2 * nr + r + 1] for r in rows])
            y_ref[0, :, g * gw:(g + 1) * gw] = yd + yoff * esc
            wsc = per_head([cols[:, nr + r:nr + r + 1] for r in rows])
            xw = (x.astype(F32) * wsc).astype(BF16)
            st = lax.dot_general(bm, xw, _TN, preferred_element_type=F32)
            dsc = per_head([elast[r:r + 1] for r in rows])
            hst_ref[u] = dsc * hprev + st


def _ssd(xbc, dtt, dtbias_col, aneg_col):
    b, s, _ = xbc.shape
    nc = s // CHUNK
    xw = SSD_GROUPS * SSD_HPG * SSD_P
    bw = SSD_GROUPS * SSD_N
    nd = dtt.shape[1]
    assert xw == 2 * bw

    def spec(width, j, rev):
        if rev:
            return pl.BlockSpec((1, CHUNK, width), lambda i, c: (i, nc - 1 - c, j))
        return pl.BlockSpec((1, CHUNK, width), lambda i, c: (i, c, j))

    dspec_f = pl.BlockSpec((1, nd, CHUNK), lambda i, c: (i, 0, c))
    dspec_b = pl.BlockSpec((1, nd, CHUNK), lambda i, c: (i, 0, nc - 1 - c))
    return pl.pallas_call(
        _ssd_kernel,
        grid=(b, nc),
        in_specs=[spec(xw, 0, False), spec(bw, 2, False), spec(bw, 3, False),
                  spec(xw, 0, True), spec(bw, 2, True), spec(bw, 3, True),
                  dspec_f, dspec_b, _const_spec(dtbias_col.shape), _const_spec(aneg_col.shape)],
        out_specs=[spec(xw, 0, False), spec(xw, 0, True)],
        out_shape=[jax.ShapeDtypeStruct((b, s, xw), F32)] * 2,
        scratch_shapes=[pltpu.VMEM((2 * SSD_GROUPS, SSD_N, SSD_HPG * SSD_P), F32)],
        compiler_params=_cparams("parallel", "arbitrary"),
        name="ssd",
    )(xbc, xbc, xbc, xbc, xbc, xbc, dtt, dtt, dtbias_col, aneg_col)


def _c_out_kernel(x_ref, yf_ref, yb_ref, xs_ref, z_ref, yc_ref, dsk_ref, ng_ref, w_ref, o_ref):
    sw = yf_ref.shape[2]
    y = yf_ref[0] + yb_ref[0] + dsk_ref[...] * xs_ref[0].astype(F32)
    y = y * _silu(z_ref[0].astype(F32))
    yn = _rms(y, ng_ref[...]).astype(BF16)
    o_ref[0] = x_ref[0] + _dot(yn, w_ref[0:sw]) + _dot(yc_ref[0], w_ref[sw:])


def _c_out(x, yf, yb, xbc, z, yc, dsk, ngain, w):
    b, s, d = x.shape
    tm = min(TM_PROJ, s)
    sw = yf.shape[2]
    cw = yc.shape[2]
    tok = lambda width: pl.BlockSpec((1, tm, width), lambda i, m: (i, m, 0))
    return pl.pallas_call(
        _c_out_kernel,
        grid=(b, s // tm),
        in_specs=[tok(d), tok(sw), tok(sw), tok(sw), tok(sw), tok(cw), _const_spec((1, sw)),
                  _const_spec((1, sw)), _const_spec(w.shape)],
        out_specs=tok(d),
        out_shape=jax.ShapeDtypeStruct((b, s, d), F32),
        compiler_params=_cparams("parallel", "parallel"),
        name="c_out",
    )(x, yf, yb, xbc, z, yc, dsk, ngain, w)


def _dft_consts(s):
    s1 = s // LANES
    c = np.arange(FNET_DIM)
    ang = 2.0 * np.pi * np.outer(c, c) / FNET_DIM
    dft = np.concatenate([np.cos(ang), -np.sin(ang)], axis=1)
    k1 = np.arange(s1)
    a1 = 2.0 * np.pi * np.outer(k1, k1) / s1
    c1, sn1 = np.cos(a1), np.sin(a1)
    w1 = np.block([[c1, sn1], [-sn1, c1]])
    s2 = np.arange(LANES)
    at = 2.0 * np.pi * (s2[:, None] * k1[None, :]) / s
    tc = np.broadcast_to(np.cos(at)[:, :, None], (LANES, s1, LANES))
    ts = np.broadcast_to(np.sin(at)[:, :, None], (LANES, s1, LANES))
    a2 = 2.0 * np.pi * np.outer(s2, s2) / LANES
    w2 = np.concatenate([np.cos(a2), np.sin(a2)], axis=1)
    scale = 1.0 / math.sqrt(s * FNET_DIM)
    return (jnp.asarray(dft, BF16), jnp.asarray(w1, BF16), jnp.asarray(tc, F32), jnp.asarray(ts, F32),
            jnp.asarray(w2, BF16), scale)


def _encode(x, p, consts):
    dft, w1, tc, ts, w2, fscale = consts
    vr, vi, qkvo, gt = _a_in(x, p["a_norm"], p["a_wf"], dft, p["a_wq"], p["a_wgt"])
    yr, yi = _fft1(vr, vi, w1, tc, ts)
    fn = _fft2(yr, yi, w2, fscale)
    hf, hb = _mlstm(qkvo, gt, p["a_gbias"])
    x = _a_out(x, fn, hf, hb, qkvo, p["a_wout"])
    x = _ffn(x, p["f_norm"][0], p["f_wu"][0], p["f_cw"][0], p["f_cb"][0], p["f_wd"][0], p["final"], False)
    zw = SSD_GROUPS * SSD_HPG * SSD_P
    z, xbc_pre, cvg, dtt = _c_in(x, p["c_norm"], p["c_w"], p["c_wdt"], zw, zw + 2 * SSD_GROUPS * SSD_N, 2 * CONF_W)
    xbc = _ssd_conv(xbc_pre, p["c_scw"], p["c_scb"])
    yc = _conf(cvg, p["c_ccw"], p["c_ccb"], p["c_lng"], p["c_lnb"])
    yf, yb = _ssd(xbc, dtt, p["c_dtbias"], p["c_aneg"])
    x = _c_out(x, yf, yb, xbc, z, yc, p["c_dsk"], p["c_sng"], p["c_wout"])
    x = _ffn(x, p["f_norm"][1], p["f_wu"][1], p["f_cw"][1], p["f_cb"][1], p["f_wd"][1], p["final"], True)
    return x


def _prep(a_norm, a_w_in, a_gate_bias, a_w_out, c_norm, c_w_in, c_ssd_conv_w, c_ssd_conv_b, c_dt_bias, c_a_log,
          c_d_skip, c_ssd_norm, c_conf_conv_w, c_conf_conv_b, c_conf_ln_g, c_conf_ln_b, c_w_out,
          ffn_norm, ffn_w_up, ffn_conv_w, ffn_conv_b, ffn_w_down, final_norm):
    fw = FNET_GROUPS * FNET_DIM
    mw = MLSTM_HEADS * MLSTM_DIM
    zw = SSD_GROUPS * SSD_HPG * SSD_P
    xw = zw + 2 * SSD_GROUPS * SSD_N
    ndt = 2 * SSD_GROUPS * SSD_HPG
    wa = a_w_in[0]
    wc = c_w_in[0]
    row = lambda v: v.reshape(1, -1).astype(F32)
    return {
        "a_norm": row(a_norm[0]),
        "a_wf": wa[:, :fw].astype(BF16),
        "a_wq": wa[:, fw:fw + 4 * mw].astype(BF16),
        "a_wgt": wa[:, fw + 4 * mw:].T.astype(BF16),
        "a_gbias": a_gate_bias[0].reshape(-1, 1).astype(F32),
        "a_wout": a_w_out[0].astype(BF16),
        "c_norm": row(c_norm[0]),
        "c_w": jnp.concatenate([wc[:, :zw + xw], wc[:, zw + xw + ndt:]], axis=1).astype(BF16),
        "c_wdt": wc[:, zw + xw:zw + xw + ndt].T.astype(BF16),
        "c_scw": c_ssd_conv_w[0].astype(F32),
        "c_scb": row(c_ssd_conv_b[0]),
        "c_dtbias": c_dt_bias[0].reshape(-1, 1).astype(F32),
        "c_aneg": (-jnp.exp(c_a_log[0].astype(F32))).reshape(-1, 1),
        "c_dsk": jnp.repeat(c_d_skip[0].reshape(-1).astype(F32), SSD_P).reshape(1, -1),
        "c_sng": row(c_ssd_norm[0]),
        "c_ccw": c_conf_conv_w[0].astype(F32),
        "c_ccb": row(c_conf_conv_b[0]),
        "c_lng": row(c_conf_ln_g[0]),
        "c_lnb": row(c_conf_ln_b[0]),
        "c_wout": c_w_out[0].astype(BF16),
        "f_norm": [row(ffn_norm[i]) for i in range(2)],
        "f_wu": [ffn_w_up[i].astype(BF16) for i in range(2)],
        "f_cw": [ffn_conv_w[i].astype(F32) for i in range(2)],
        "f_cb": [row(ffn_conv_b[i]) for i in range(2)],
        "f_wd": [ffn_w_down[i].astype(BF16) for i in range(2)],
        "final": row(final_norm),
    }


def kernel(x_prompt, x_sample, a_norm, a_w_in, a_gate_bias, a_w_out, c_norm, c_w_in, c_ssd_conv_w, c_ssd_conv_b, c_dt_bias, c_a_log, c_d_skip, c_ssd_norm, c_conf_conv_w, c_conf_conv_b, c_conf_ln_g, c_conf_ln_b, c_w_out, ffn_norm, ffn_w_up, ffn_conv_w, ffn_conv_b, ffn_w_down, final_norm):
    assert a_norm.shape[0] == 1 and c_norm.shape[0] == 1 and ffn_norm.shape[0] == 2
    p = _prep(a_norm, a_w_in, a_gate_bias, a_w_out, c_norm, c_w_in, c_ssd_conv_w, c_ssd_conv_b, c_dt_bias, c_a_log,
              c_d_skip, c_ssd_norm, c_conf_conv_w, c_conf_conv_b, c_conf_ln_g, c_conf_ln_b, c_w_out,
              ffn_norm, ffn_w_up, ffn_conv_w, ffn_conv_b, ffn_w_down, final_norm)
    outs = []
    consts = {}
    for x in (x_prompt, x_sample):
        s = x.shape[1]
        assert s % (LANES * SUBLANES) == 0
        if s not in consts:
            consts[s] = _dft_consts(s)
        outs.append(_encode(x.astype(F32), p, consts[s]))
    return tuple(outs)
```

```python
import functools
import math

import numpy as np
import jax
import jax.numpy as jnp
from jax import lax
from jax.experimental import pallas as pl
from jax.experimental.pallas import tpu as pltpu

F32 = jnp.float32
BF16 = jnp.bfloat16
EPS = 1e-6

LANES = 128
SUBLANES = 8
CHUNK = 128
VMEM_LIMIT = 56 * 1024 * 1024

FNET_GROUPS = 4
FNET_DIM = 128
MLSTM_HEADS = 4
MLSTM_DIM = 128
SSD_GROUPS = 4
SSD_HPG = 4
SSD_P = 64
SSD_N = 128
CONF_W = 512

TM_PROJ = 512
TM_FFN = 512
TM_CONV = 512
HALO = 8
HALO_CONF = 16
FFN_HC = 512

_NT = (((1,), (1,)), ((), ()))
_TN = (((0,), (0,)), ((), ()))


def _cparams(*sem):
    return pltpu.CompilerParams(dimension_semantics=sem, vmem_limit_bytes=VMEM_LIMIT)


def _const_spec(shape):
    nd = len(shape)
    return pl.BlockSpec(shape, lambda *_: (0,) * nd, pipeline_mode=pl.Buffered(1))


def _dot(a, b):
    return jnp.dot(a, b, preferred_element_type=F32)


def _rms(x, g):
    return x * lax.rsqrt(jnp.mean(x * x, axis=-1, keepdims=True) + EPS) * g


def _softplus(x):
    return jnp.maximum(x, 0.0) + jnp.log1p(jnp.exp(-jnp.abs(x)))


def _log_sigmoid(x):
    return -_softplus(-x)


def _silu(x):
    return x * jax.nn.sigmoid(x)


def _scan_lanes(x, combine, fill):
    ax = x.ndim - 1
    lane = lax.broadcasted_iota(jnp.int32, x.shape, ax)
    pre, suf = x, x
    k = 1
    while k < CHUNK:
        pre = combine(pre, jnp.where(lane >= k, pltpu.roll(pre, k, axis=ax), fill))
        suf = combine(suf, jnp.where(lane < CHUNK - k, pltpu.roll(suf, CHUNK - k, axis=ax), fill))
        k *= 2
    return pre, suf


def _cols_out(cols_ref, stk_scr):
    def body(c, carry):
        cols_ref[0, c] = stk_scr[c].T
        return carry
    lax.fori_loop(0, stk_scr.shape[0], body, 0)


def _chunk_specs(shape, nc):
    fwd = pl.BlockSpec((1, 1) + shape, lambda i, c: (i, c, 0, 0))
    bwd = pl.BlockSpec((1, 1) + shape, lambda i, c: (i, nc - 1 - c, 0, 0))
    return fwd, bwd


def _a_in_kernel(x_ref, g_ref, wf_ref, dft_ref, wq_ref, wkt_ref, wgt_ref, vr_ref, vi_ref, qvo_ref, kt_ref, gt_ref):
    xb = _rms(x_ref[0], g_ref[...]).astype(BF16)
    hf = _dot(xb, wf_ref[...]).astype(BF16)
    for g in range(FNET_GROUPS):
        sl = slice(g * FNET_DIM, (g + 1) * FNET_DIM)
        pq = _dot(hf[:, sl], dft_ref[...])
        vr_ref[0, :, sl] = pq[:, :FNET_DIM].astype(BF16)
        vi_ref[0, :, sl] = pq[:, FNET_DIM:].astype(BF16)
    mw = MLSTM_HEADS * MLSTM_DIM
    qvo_ref[0, :, 0:mw] = (_dot(xb, wq_ref[:, 0:mw]) * (MLSTM_DIM ** -0.5)).astype(BF16)
    for j in range(1, qvo_ref.shape[2] // mw):
        sl = slice(j * mw, (j + 1) * mw)
        qvo_ref[0, :, sl] = _dot(xb, wq_ref[:, sl]).astype(BF16)
    kt_ref[0] = lax.dot_general(wkt_ref[...], xb, _NT, preferred_element_type=F32).astype(BF16)
    gt = lax.dot_general(wgt_ref[...], xb, _NT, preferred_element_type=F32)
    for c in range(gt_ref.shape[1]):
        gt_ref[0, c] = gt[:, c * CHUNK:(c + 1) * CHUNK]


def _a_in(x, g, wf, dft, wq, wkt, wgt):
    b, s, d = x.shape
    tm = min(TM_PROJ, s)
    fw = wf.shape[1]
    nq = wq.shape[1]
    kw = wkt.shape[0]
    ng = wgt.shape[0]
    return pl.pallas_call(
        _a_in_kernel,
        grid=(b, s // tm),
        in_specs=[
            pl.BlockSpec((1, tm, d), lambda i, m: (i, m, 0)),
            _const_spec((1, d)), _const_spec(wf.shape), _const_spec(dft.shape),
            _const_spec(wq.shape), _const_spec(wkt.shape), _const_spec(wgt.shape),
        ],
        out_specs=[
            pl.BlockSpec((1, tm, fw), lambda i, m: (i, m, 0)),
            pl.BlockSpec((1, tm, fw), lambda i, m: (i, m, 0)),
            pl.BlockSpec((1, tm, nq), lambda i, m: (i, m, 0)),
            pl.BlockSpec((1, kw, tm), lambda i, m: (i, 0, m)),
            pl.BlockSpec((1, tm // CHUNK, ng, CHUNK), lambda i, m: (i, m, 0, 0)),
        ],
        out_shape=[
            jax.ShapeDtypeStruct((b, s, fw), BF16),
            jax.ShapeDtypeStruct((b, s, fw), BF16),
            jax.ShapeDtypeStruct((b, s, nq), BF16),
            jax.ShapeDtypeStruct((b, kw, s), BF16),
            jax.ShapeDtypeStruct((b, s // CHUNK, ng, CHUNK), F32),
        ],
        compiler_params=_cparams("parallel", "parallel"),
        name="a_in",
    )(x, g, wf, dft, wq, wkt, wgt)


def _fft1_kernel(vr_ref, vi_ref, w1_ref, tc_ref, ts_ref, yr_ref, yi_ref):
    s1 = vr_ref.shape[1]
    ts2 = tc_ref.shape[0]
    cw = yr_ref.shape[3]
    v = jnp.concatenate([vr_ref[0], vi_ref[0]], axis=0)
    y = _dot(w1_ref[...], v)
    for i in range(ts2):
        tc = tc_ref[i]
        ts = ts_ref[i]
        for g in range(cw // LANES):
            sl = slice(i * cw + g * LANES, i * cw + (g + 1) * LANES)
            osl = slice(g * LANES, (g + 1) * LANES)
            yr = y[:s1, sl]
            yi = y[s1:, sl]
            yr_ref[0, i, :, osl] = (yr * tc + yi * ts).astype(BF16)
            yi_ref[0, i, :, osl] = (yi * tc - yr * ts).astype(BF16)


def _fft1(vr, vi, w1, tc, ts):
    b, s, cw = vr.shape
    s1 = s // LANES
    ts2 = 8
    vr2 = vr.reshape(b, s1, LANES * cw)
    vi2 = vi.reshape(b, s1, LANES * cw)
    spec_in = pl.BlockSpec((1, s1, ts2 * cw), lambda j, i: (i, 0, j))
    spec_tw = pl.BlockSpec((ts2, s1, LANES), lambda j, i: (j, 0, 0))
    spec_out = pl.BlockSpec((1, ts2, s1, cw), lambda j, i: (i, j, 0, 0))
    return pl.pallas_call(
        _fft1_kernel,
        grid=(LANES // ts2, b),
        in_specs=[spec_in, spec_in, _const_spec(w1.shape), spec_tw, spec_tw],
        out_specs=[spec_out, spec_out],
        out_shape=[jax.ShapeDtypeStruct((b, LANES, s1, cw), BF16)] * 2,
        compiler_params=_cparams("parallel", "parallel"),
        name="fft1",
    )(vr2, vi2, w1, tc, ts)


def _fft2_kernel(yr_ref, yi_ref, w2_ref, o_ref, *, scale):
    y = jnp.concatenate([yr_ref[0], yi_ref[0]], axis=0)
    o_ref[0] = (_dot(w2_ref[...], y) * scale).astype(BF16)


def _fft2(yr, yi, w2, scale):
    b, _, s1, cw = yr.shape
    n = s1 * cw
    tn = min(4096, n)
    yr2 = yr.reshape(b, LANES, n)
    yi2 = yi.reshape(b, LANES, n)
    spec = pl.BlockSpec((1, LANES, tn), lambda i, j: (i, 0, j))
    out = pl.pallas_call(
        functools.partial(_fft2_kernel, scale=scale),
        grid=(b, n // tn),
        in_specs=[spec, spec, _const_spec(w2.shape)],
        out_specs=spec,
        out_shape=jax.ShapeDtypeStruct((b, LANES, n), BF16),
        compiler_params=_cparams("parallel", "parallel"),
        name="fft2",
    )(yr2, yi2, w2)
    return out.reshape(b, LANES * s1, cw)


def _mlstm_gates_kernel(gt_ref, bias_ref, rows_ref, cols_ref, tot_scr, amax_scr, mpf_scr, mpb_scr, stk_scr):
    nh = MLSTM_HEADS
    nu = 2 * nh
    nc = gt_ref.shape[1]
    g = gt_ref[0]
    bias = bias_ref[...][None]
    li = g[:, 0:nu] + bias[:, 0:nu]
    lf = _log_sigmoid(g[:, nu:] + bias[:, nu:])
    isf = lax.broadcasted_iota(jnp.int32, li.shape, 1) < nh
    pre, suf = _scan_lanes(lf, jnp.add, 0.0)
    bcs = jnp.where(isf, pre, suf)
    tot = pre + suf - lf
    e = li - bcs
    a = tot - bcs + li
    pa, sa = _scan_lanes(a, jnp.maximum, -jnp.inf)
    amax = jnp.maximum(pa, sa)
    pe, se = _scan_lanes(e, jnp.maximum, -jnp.inf)
    cmax = jnp.where(isf, pe, se)
    tot_scr[...] = tot
    amax_scr[...] = amax

    def fwd(c, m):
        mpf_scr[c] = m
        return jnp.maximum(tot_scr[c] + m, amax_scr[c])

    def bwd(i, m):
        c = nc - 1 - i
        mpb_scr[c] = m
        return jnp.maximum(tot_scr[c] + m, amax_scr[c])

    zero = jnp.zeros((nu, CHUNK), F32)
    lax.fori_loop(0, nc, fwd, zero)
    lax.fori_loop(0, nc, bwd, zero)
    m_prev = jnp.where(isf, mpf_scr[...], mpb_scr[...])
    m_new = jnp.maximum(tot + m_prev, amax)
    mu = jnp.maximum(m_prev, cmax)
    rows_ref[0, :, 0:nu] = e
    rows_ref[0, :, nu:2 * nu] = jnp.exp(a - m_new)
    rows_ref[0, :, 2 * nu:3 * nu] = jnp.exp(tot + m_prev - m_new)
    stk_scr[:, 0:nu] = mu
    stk_scr[:, nu:2 * nu] = jnp.exp(m_prev - mu)
    stk_scr[:, 2 * nu:3 * nu] = jnp.exp(-(bcs + mu))
    stk_scr[:, 3 * nu:] = jnp.zeros((nc, CHUNK - 3 * nu, CHUNK), F32)
    _cols_out(cols_ref, stk_scr)


def _mlstm_gates(gt, bias_col):
    b, nc, ng, _ = gt.shape
    nu = ng // 2
    small = pltpu.VMEM((nc, nu, CHUNK), F32)
    return pl.pallas_call(
        _mlstm_gates_kernel,
        grid=(b,),
        in_specs=[pl.BlockSpec((1, nc, ng, CHUNK), lambda i: (i, 0, 0, 0)), _const_spec(bias_col.shape)],
        out_specs=[pl.BlockSpec((1, nc, 3 * nu, CHUNK), lambda i: (i, 0, 0, 0)),
                   pl.BlockSpec((1, nc, CHUNK, CHUNK), lambda i: (i, 0, 0, 0))],
        out_shape=[jax.ShapeDtypeStruct((b, nc, 3 * nu, CHUNK), F32),
                   jax.ShapeDtypeStruct((b, nc, CHUNK, CHUNK), F32)],
        scratch_shapes=[small, small, small, small, pltpu.VMEM((nc, CHUNK, CHUNK), F32)],
        compiler_params=_cparams("parallel"),
        name="mlstm_gates",
    )(gt, bias_col)


def _mlstm_kernel(qf_ref, ktf_ref, vf_ref, qb_ref, ktb_ref, vb_ref, rf_ref, rb_ref, cf_ref, cb_ref,
                  hf_ref, hb_ref, cst_ref):
    nh, dh = MLSTM_HEADS, MLSTM_DIM
    nu = 2 * nh
    L = CHUNK

    @pl.when(pl.program_id(1) == 0)
    def _():
        cst_ref[...] = jnp.zeros_like(cst_ref)

    t_i = lax.broadcasted_iota(jnp.int32, (L, L), 0)
    s_i = lax.broadcasted_iota(jnp.int32, (L, L), 1)
    top = lax.broadcasted_iota(jnp.int32, (2 * dh, L), 0) < dh
    ones = jnp.ones((L, dh), BF16)
    for d in range(2):
        q_ref, kt_ref, v_ref, r_ref, c_ref, o_ref = ((qf_ref, ktf_ref, vf_ref, rf_ref, cf_ref, hf_ref) if d == 0 else
                                                     (qb_ref, ktb_ref, vb_ref, rb_ref, cb_ref, hb_ref))
        tri = (s_i <= t_i) if d == 0 else (s_i >= t_i)
        for hp in range(nh // 2):
            psl = slice(2 * hp * dh, (2 * hp + 2) * dh)
            kt2 = kt_ref[0, psl, :]
            zero = jnp.zeros_like(kt2)
            kbd = jnp.concatenate([jnp.where(top, kt2, zero), jnp.where(top, zero, kt2)], axis=1)
            sc2 = _dot(q_ref[0, :, psl], kbd)
            for hh in range(2):
                h = 2 * hp + hh
                u = d * nh + h
                sl = slice(h * dh, (h + 1) * dh)
                q = q_ref[0, :, sl]
                v = v_ref[0, :, sl]
                vaug = jnp.concatenate([v, ones], axis=1)
                e_r = r_ref[0, 0, u:u + 1, :]
                ea_r = r_ref[0, 0, nu + u:nu + u + 1, :]
                decay = r_ref[0, 0, 2 * nu + u:2 * nu + u + 1, 0:1]
                mu_c = c_ref[0, 0, :, u:u + 1]
                isc_c = c_ref[0, 0, :, nu + u:nu + u + 1]
                emt_c = c_ref[0, 0, :, 2 * nu + u:2 * nu + u + 1]
                w = jnp.exp(jnp.where(tri, e_r - mu_c, -jnp.inf))
                sw = (sc2[:, hh * L:(hh + 1) * L] * w).astype(BF16)
                qi = (q.astype(F32) * isc_c).astype(BF16)
                cs = cst_ref[u]
                tot = _dot(jnp.concatenate([sw, qi], axis=1),
                           jnp.concatenate([vaug, cs.astype(BF16)], axis=0))
                den = jnp.maximum(jnp.abs(tot[:, dh:]), emt_c)
                o_ref[0, :, sl] = (tot[:, :dh] / den).astype(o_ref.dtype)
                kw = (kt_ref[0, sl, :].astype(F32) * ea_r).astype(BF16)
                cst_ref[u] = decay * cs + _dot(kw, vaug)


def _mlstm(qvo, kt, rows, cols):
    b, s, _ = qvo.shape
    nc = s // CHUNK
    w = MLSTM_HEADS * MLSTM_DIM
    nr = rows.shape[2]

    def tok(j, rev):
        if rev:
            return pl.BlockSpec((1, CHUNK, w), lambda i, c: (i, nc - 1 - c, j))
        return pl.BlockSpec((1, CHUNK, w), lambda i, c: (i, c, j))

    ktf = pl.BlockSpec((1, w, CHUNK), lambda i, c: (i, 0, c))
    ktb = pl.BlockSpec((1, w, CHUNK), lambda i, c: (i, 0, nc - 1 - c))
    return pl.pallas_call(
        _mlstm_kernel,
        grid=(b, nc),
        in_specs=[tok(0, False), ktf, tok(1, False), tok(0, True), ktb, tok(1, True),
                  *_chunk_specs((nr, CHUNK), nc), *_chunk_specs((CHUNK, CHUNK), nc)],
        out_specs=[tok(0, False), tok(0, True)],
        out_shape=[jax.ShapeDtypeStruct((b, s, w), BF16)] * 2,
        scratch_shapes=[pltpu.VMEM((2 * MLSTM_HEADS, MLSTM_DIM, 2 * MLSTM_DIM), F32)],
        compiler_params=_cparams("parallel", "arbitrary"),
        name="mlstm",
    )(qvo, kt, qvo, qvo, kt, qvo, rows, rows, cols, cols)


def _a_out_kernel(x_ref, fn_ref, hf_ref, hb_ref, o_ref, w_ref, y_ref):
    fw = fn_ref.shape[2]
    mix = (jax.nn.sigmoid(o_ref[0].astype(F32)) * (hf_ref[0].astype(F32) + hb_ref[0].astype(F32))).astype(BF16)
    y_ref[0] = x_ref[0] + _dot(fn_ref[0], w_ref[0:fw]) + _dot(mix, w_ref[fw:])


def _a_out(x, fn, hf, hb, qvo, w):
    b, s, d = x.shape
    tm = min(TM_PROJ, s)
    fw = fn.shape[2]
    mw = hf.shape[2]
    tok = lambda width, j=0: pl.BlockSpec((1, tm, width), lambda i, m: (i, m, j))
    return pl.pallas_call(
        _a_out_kernel,
        grid=(b, s // tm),
        in_specs=[tok(d), tok(fw), tok(mw), tok(mw), tok(mw, 2), _const_spec(w.shape)],
        out_specs=tok(d),
        out_shape=jax.ShapeDtypeStruct((b, s, d), F32),
        compiler_params=_cparams("parallel", "parallel"),
        name="a_out",
    )(x, fn, hf, hb, qvo, w)


def _halo_specs(tm, halo, width, s):
    r = tm // halo
    nb = s // halo
    mid = pl.BlockSpec((1, tm, width), lambda i, m: (i, m, 0))
    top = pl.BlockSpec((1, halo, width), lambda i, m: (i, jnp.maximum(m * r - 1, 0), 0))
    bot = pl.BlockSpec((1, halo, width), lambda i, m: (i, jnp.minimum((m + 1) * r, nb - 1), 0))
    return mid, top, bot


def _ffn_kernel(x_ref, top_ref, bot_ref, g_ref, wu_ref, cw_ref, cb_ref, wd_ref, fg_ref, y_ref, a_scr, *, final_norm):
    m = pl.program_id(1)
    tm = x_ref.shape[1]
    fh = wd_ref.shape[0]
    g = g_ref[...]
    x = x_ref[0]
    top = jnp.where(m > 0, _rms(top_ref[0], g), 0.0)
    bot = jnp.where(m < pl.num_programs(1) - 1, _rms(bot_ref[0], g), 0.0)
    xe = jnp.concatenate([top, _rms(x, g), bot], axis=0).astype(BF16)
    n_ext = tm + 2 * HALO

    def conv3(h, c0):
        w = cw_ref[:, c0:c0 + FFN_HC]
        out = (w[0:1] * pltpu.roll(h, 1, axis=0) + w[1:2] * h + w[2:3] * pltpu.roll(h, n_ext - 1, axis=0))
        return out[HALO:HALO + tm] + cb_ref[:, c0:c0 + FFN_HC]

    for j in range(fh // FFN_HC):
        c0 = j * FFN_HC
        gate = conv3(_dot(xe, wu_ref[:, c0:c0 + FFN_HC]), c0)
        val = conv3(_dot(xe, wu_ref[:, fh + c0:fh + c0 + FFN_HC]), fh + c0)
        a_scr[:, c0:c0 + FFN_HC] = (_silu(gate) * val).astype(BF16)
    y = x + _dot(a_scr[...], wd_ref[...])
    if final_norm:
        y = _rms(y, fg_ref[...])
    y_ref[0] = y


def _ffn(x, g, wu, cw, cb, wd, fg, final_norm):
    b, s, d = x.shape
    tm = min(TM_FFN, s)
    fh = wd.shape[0]
    mid, top, bot = _halo_specs(tm, HALO, d, s)
    return pl.pallas_call(
        functools.partial(_ffn_kernel, final_norm=final_norm),
        grid=(b, s // tm),
        in_specs=[mid, top, bot, _const_spec((1, d)), _const_spec(wu.shape), _const_spec(cw.shape),
                  _const_spec(cb.shape), _const_spec(wd.shape), _const_spec((1, d))],
        out_specs=mid,
        out_shape=jax.ShapeDtypeStruct((b, s, d), F32),
        scratch_shapes=[pltpu.VMEM((tm, fh), BF16)],
        compiler_params=_cparams("parallel", "parallel"),
        name="ffn",
    )(x, x, x, g, wu, cw, cb, wd, fg)


def _c_in_kernel(x_ref, g_ref, w_ref, wdt_ref, z_ref, xbc_ref, cvg_ref, dtt_ref):
    xb = _rms(x_ref[0], g_ref[...]).astype(BF16)
    zw = z_ref.shape[2]
    xw = xbc_ref.shape[2]
    cw = cvg_ref.shape[2]
    for j in range(zw // 512):
        sl = slice(j * 512, (j + 1) * 512)
        z_ref[0, :, sl] = _dot(xb, w_ref[:, sl]).astype(BF16)
    for j in range(xw // 512):
        sl = slice(j * 512, (j + 1) * 512)
        xbc_ref[0, :, sl] = _dot(xb, w_ref[:, zw + j * 512:zw + (j + 1) * 512])
    for j in range(cw // 512):
        sl = slice(j * 512, (j + 1) * 512)
        cvg_ref[0, :, sl] = _dot(xb, w_ref[:, zw + xw + j * 512:zw + xw + (j + 1) * 512])
    dtt = lax.dot_general(wdt_ref[...], xb, _NT, preferred_element_type=F32)
    for c in range(dtt_ref.shape[1]):
        dtt_ref[0, c] = dtt[:, c * CHUNK:(c + 1) * CHUNK]


def _c_in(x, g, w, wdt, zw, xw, cw):
    b, s, d = x.shape
    tm = min(TM_PROJ, s)
    nd = wdt.shape[0]
    tok = lambda width: pl.BlockSpec((1, tm, width), lambda i, m: (i, m, 0))
    return pl.pallas_call(
        _c_in_kernel,
        grid=(b, s // tm),
        in_specs=[tok(d), _const_spec((1, d)), _const_spec(w.shape), _const_spec(wdt.shape)],
        out_specs=[tok(zw), tok(xw), tok(cw),
                   pl.BlockSpec((1, tm // CHUNK, nd, CHUNK), lambda i, m: (i, m, 0, 0))],
        out_shape=[jax.ShapeDtypeStruct((b, s, zw), BF16), jax.ShapeDtypeStruct((b, s, xw), F32),
                   jax.ShapeDtypeStruct((b, s, cw), F32), jax.ShapeDtypeStruct((b, s // CHUNK, nd, CHUNK), F32)],
        compiler_params=_cparams("parallel", "parallel"),
        name="c_in",
    )(x, g, w, wdt)


def _dwconv_taps(xe, w_ref, tm, halo):
    k = w_ref.shape[0]
    pad = (k - 1) // 2
    n = xe.shape[0]
    acc = None
    for j in range(k):
        off = halo - pad + j
        tap = w_ref[j:j + 1, :] * pltpu.roll(xe, (n - off) % n, axis=0)[0:tm]
        acc = tap if acc is None else acc + tap
    return acc


def _ssd_conv_kernel(x_ref, top_ref, bot_ref, w_ref, b_ref, o_ref):
    m = pl.program_id(1)
    tm = x_ref.shape[1]
    top = jnp.where(m > 0, top_ref[0], 0.0)
    bot = jnp.where(m < pl.num_programs(1) - 1, bot_ref[0], 0.0)
    xe = jnp.concatenate([top, x_ref[0], bot], axis=0)
    o_ref[0] = _silu(_dwconv_taps(xe, w_ref, tm, HALO) + b_ref[...]).astype(o_ref.dtype)


def _ssd_conv(x, w, bias):
    b, s, c = x.shape
    tm = min(TM_CONV, s)
    tc = 512
    r = tm // HALO
    nb = s // HALO
    mid = pl.BlockSpec((1, tm, tc), lambda i, m, j: (i, m, j))
    top = pl.BlockSpec((1, HALO, tc), lambda i, m, j: (i, jnp.maximum(m * r - 1, 0), j))
    bot = pl.BlockSpec((1, HALO, tc), lambda i, m, j: (i, jnp.minimum((m + 1) * r, nb - 1), j))
    return pl.pallas_call(
        _ssd_conv_kernel,
        grid=(b, s // tm, c // tc),
        in_specs=[mid, top, bot, pl.BlockSpec((w.shape[0], tc), lambda i, m, j: (0, j)),
                  pl.BlockSpec((1, tc), lambda i, m, j: (0, j))],
        out_specs=mid,
        out_shape=jax.ShapeDtypeStruct((b, s, c), BF16),
        compiler_params=_cparams("parallel", "parallel", "parallel"),
        name="ssd_conv",
    )(x, x, x, w, bias)


def _conf_kernel(x_ref, top_ref, bot_ref, w_ref, b_ref, lg_ref, lb_ref, o_ref):
    m = pl.program_id(1)
    tm = x_ref.shape[1]
    cw = o_ref.shape[2]

    def glu(t):
        return t[:, :cw] * jax.nn.sigmoid(t[:, cw:])

    top = jnp.where(m > 0, glu(top_ref[0]), 0.0)
    bot = jnp.where(m < pl.num_programs(1) - 1, glu(bot_ref[0]), 0.0)
    xe = jnp.concatenate([top, glu(x_ref[0]), bot], axis=0)
    u = _dwconv_taps(xe, w_ref, tm, HALO_CONF) + b_ref[...]
    mu = jnp.mean(u, axis=-1, keepdims=True)
    uc = u - mu
    var = jnp.mean(uc * uc, axis=-1, keepdims=True)
    o_ref[0] = _silu(uc * lax.rsqrt(var + EPS) * lg_ref[...] + lb_ref[...]).astype(o_ref.dtype)


def _conf(cvg, w, bias, lg, lb):
    b, s, c2 = cvg.shape
    cw = c2 // 2
    tm = min(256, s)
    mid, top, bot = _halo_specs(tm, HALO_CONF, c2, s)
    return pl.pallas_call(
        _conf_kernel,
        grid=(b, s // tm),
        in_specs=[mid, top, bot, _const_spec(w.shape), _const_spec((1, cw)), _const_spec((1, cw)),
                  _const_spec((1, cw))],
        out_specs=pl.BlockSpec((1, tm, cw), lambda i, m: (i, m, 0)),
        out_shape=jax.ShapeDtypeStruct((b, s, cw), BF16),
        compiler_params=_cparams("parallel", "parallel"),
        name="conf_conv",
    )(cvg, cvg, cvg, w, bias, lg, lb)


def _ssd_gates_kernel(dt_ref, bias_ref, aneg_ref, rows_ref, cols_ref, stk_scr):
    nh = SSD_GROUPS * SSD_HPG
    nr = 2 * nh
    nc = dt_ref.shape[1]
    dt = _softplus(dt_ref[0] + bias_ref[...][None])
    da = dt * aneg_ref[...][None]
    isf = lax.broadcasted_iota(jnp.int32, dt.shape, 1) < nh
    pre, suf = _scan_lanes(da, jnp.add, 0.0)
    acs = jnp.where(isf, pre, suf)
    tot = pre + suf - da
    rows_ref[0, :, 0:nr] = dt
    rows_ref[0, :, nr:2 * nr] = acs
    rows_ref[0, :, 2 * nr:3 * nr] = jnp.exp(tot - acs) * dt
    rows_ref[0, :, 3 * nr:4 * nr] = jnp.exp(tot)
    stk_scr[:, 0:nr] = acs
    stk_scr[:, nr:] = jnp.zeros((nc, CHUNK - nr, CHUNK), F32)
    _cols_out(cols_ref, stk_scr)


def _ssd_gates(dtt, dtbias_col, aneg_col):
    b, nc, nr, _ = dtt.shape
    assert 4 * nr == CHUNK
    full = pl.BlockSpec((1, nc, CHUNK, CHUNK), lambda i: (i, 0, 0, 0))
    return pl.pallas_call(
        _ssd_gates_kernel,
        grid=(b,),
        in_specs=[pl.BlockSpec((1, nc, nr, CHUNK), lambda i: (i, 0, 0, 0)), _const_spec(dtbias_col.shape),
                  _const_spec(aneg_col.shape)],
        out_specs=[full, full],
        out_shape=[jax.ShapeDtypeStruct((b, nc, CHUNK, CHUNK), F32)] * 2,
        scratch_shapes=[pltpu.VMEM((nc, CHUNK, CHUNK), F32)],
        compiler_params=_cparams("parallel"),
        name="ssd_gates",
    )(dtt, dtbias_col, aneg_col)


def _ssd_kernel(xf_ref, bf_ref, cf_ref, xb_ref, bb_ref, cb_ref, rf_ref, rb_ref, kf_ref, kb_ref,
                yf_ref, yb_ref, hst_ref):
    ng, nj, hp, ns = SSD_GROUPS, SSD_HPG, SSD_P, SSD_N
    nh = ng * nj
    nr = 2 * nh
    gw = nj * hp
    L = CHUNK

    @pl.when(pl.program_id(1) == 0)
    def _():
        hst_ref[...] = jnp.zeros_like(hst_ref)

    t_i = lax.broadcasted_iota(jnp.int32, (L, L), 0)
    s_i = lax.broadcasted_iota(jnp.int32, (L, L), 1)
    hid = lax.broadcasted_iota(jnp.int32, (1, gw), 1) // hp
    hmask = [(hid == j).astype(F32).astype(BF16) for j in range(nj)]
    low = lax.broadcasted_iota(jnp.int32, (1, 2 * hp), 1) < hp

    def pair_lanes(vals):
        return jnp.concatenate([jnp.where(low, vals[2 * i], vals[2 * i + 1]) for i in range(nj // 2)], axis=1)

    for d in range(2):
        x_ref, b_ref, c_ref, r_ref, k_ref, y_ref = ((xf_ref, bf_ref, cf_ref, rf_ref, kf_ref, yf_ref) if d == 0 else
                                                    (xb_ref, bb_ref, cb_ref, rb_ref, kb_ref, yb_ref))
        tri = (s_i <= t_i) if d == 0 else (s_i >= t_i)
        for g in range(ng):
            u = d * ng + g
            rows = [d * nh + g * nj + j for j in range(nj)]
            x = x_ref[0, :, g * gw:(g + 1) * gw]
            bm = b_ref[0, :, g * ns:(g + 1) * ns]
            cm = c_ref[0, :, g * ns:(g + 1) * ns]
            cbm = lax.dot_general(cm, bm, _NT, preferred_element_type=F32)
            bmt = bm.T.astype(F32)
            yd = None
            st = None
            esc = []
            for i in range(nj // 2):
                ws, bws, xs = [], [], []
                for j in (2 * i, 2 * i + 1):
                    r = rows[j]
                    acs_c = jnp.broadcast_to(k_ref[0, 0, :, r:r + 1], (L, L))
                    dec = jnp.exp(jnp.where(tri, acs_c - r_ref[0, 0, nr + r:nr + r + 1, :], -jnp.inf))
                    ws.append((cbm * dec * r_ref[0, 0, r:r + 1, :]).astype(BF16))
                    bws.append((bmt * r_ref[0, 0, 2 * nr + r:2 * nr + r + 1, :]).astype(BF16))
                    xs.append(x * hmask[j])
                    esc.append(jnp.exp(acs_c))
                xk = jnp.concatenate(xs, axis=0)
                part = _dot(jnp.concatenate(ws, axis=1), xk)
                spart = _dot(jnp.concatenate(bws, axis=1), xk)
                yd = part if yd is None else yd + part
                st = spart if st is None else st + spart
            hprev = hst_ref[u]
            yoff = _dot(cm, hprev.astype(BF16))
            y_ref[0, :, g * gw:(g + 1) * gw] = (yd + yoff * pair_lanes(esc)).astype(y_ref.dtype)
            dsc = pair_lanes([r_ref[0, 0, 3 * nr + r:3 * nr + r + 1, :] for r in rows])
            hst_ref[u] = dsc * hprev + st


def _ssd(xbc, rows, cols):
    b, s, _ = xbc.shape
    nc = s // CHUNK
    xw = SSD_GROUPS * SSD_HPG * SSD_P
    bw = SSD_GROUPS * SSD_N
    assert xw == 2 * bw and 2 * SSD_P == LANES

    def spec(width, j, rev):
        if rev:
            return pl.BlockSpec((1, CHUNK, width), lambda i, c: (i, nc - 1 - c, j))
        return pl.BlockSpec((1, CHUNK, width), lambda i, c: (i, c, j))

    return pl.pallas_call(
        _ssd_kernel,
        grid=(b, nc),
        in_specs=[spec(xw, 0, False), spec(bw, 2, False), spec(bw, 3, False),
                  spec(xw, 0, True), spec(bw, 2, True), spec(bw, 3, True),
                  *_chunk_specs((CHUNK, CHUNK), nc), *_chunk_specs((CHUNK, CHUNK), nc)],
        out_specs=[spec(xw, 0, False), spec(xw, 0, True)],
        out_shape=[jax.ShapeDtypeStruct((b, s, xw), BF16)] * 2,
        scratch_shapes=[pltpu.VMEM((2 * SSD_GROUPS, SSD_N, SSD_HPG * SSD_P), F32)],
        compiler_params=_cparams("parallel", "arbitrary"),
        name="ssd",
    )(xbc, xbc, xbc, xbc, xbc, xbc, rows, rows, cols, cols)


def _c_out_kernel(x_ref, yf_ref, yb_ref, xs_ref, z_ref, yc_ref, dsk_ref, ng_ref, w_ref, o_ref):
    sw = yf_ref.shape[2]
    y = yf_ref[0].astype(F32) + yb_ref[0].astype(F32) + dsk_ref[...] * xs_ref[0].astype(F32)
    y = y * _silu(z_ref[0].astype(F32))
    yn = _rms(y, ng_ref[...]).astype(BF16)
    o_ref[0] = x_ref[0] + _dot(yn, w_ref[0:sw]) + _dot(yc_ref[0], w_ref[sw:])


def _c_out(x, yf, yb, xbc, z, yc, dsk, ngain, w):
    b, s, d = x.shape
    tm = min(TM_PROJ, s)
    sw = yf.shape[2]
    cw = yc.shape[2]
    tok = lambda width: pl.BlockSpec((1, tm, width), lambda i, m: (i, m, 0))
    return pl.pallas_call(
        _c_out_kernel,
        grid=(b, s // tm),
        in_specs=[tok(d), tok(sw), tok(sw), tok(sw), tok(sw), tok(cw), _const_spec((1, sw)),
                  _const_spec((1, sw)), _const_spec(w.shape)],
        out_specs=tok(d),
        out_shape=jax.ShapeDtypeStruct((b, s, d), F32),
        compiler_params=_cparams("parallel", "parallel"),
        name="c_out",
    )(x, yf, yb, xbc, z, yc, dsk, ngain, w)


def _dft_consts(s):
    s1 = s // LANES
    c = np.arange(FNET_DIM)
    ang = 2.0 * np.pi * np.outer(c, c) / FNET_DIM
    dft = np.concatenate([np.cos(ang), -np.sin(ang)], axis=1)
    k1 = np.arange(s1)
    a1 = 2.0 * np.pi * np.outer(k1, k1) / s1
    c1, sn1 = np.cos(a1), np.sin(a1)
    w1 = np.block([[c1, sn1], [-sn1, c1]])
    s2 = np.arange(LANES)
    at = 2.0 * np.pi * (s2[:, None] * k1[None, :]) / s
    tc = np.broadcast_to(np.cos(at)[:, :, None], (LANES, s1, LANES))
    ts = np.broadcast_to(np.sin(at)[:, :, None], (LANES, s1, LANES))
    a2 = 2.0 * np.pi * np.outer(s2, s2) / LANES
    w2 = np.concatenate([np.cos(a2), np.sin(a2)], axis=1)
    scale = 1.0 / math.sqrt(s * FNET_DIM)
    return (jnp.asarray(dft, BF16), jnp.asarray(w1, BF16), jnp.asarray(tc, F32), jnp.asarray(ts, F32),
            jnp.asarray(w2, BF16), scale)


def _encode(x, p, consts):
    dft, w1, tc, ts, w2, fscale = consts
    vr, vi, qvo, kt, gt = _a_in(x, p["a_norm"], p["a_wf"], dft, p["a_wq"], p["a_wkt"], p["a_wgt"])
    yr, yi = _fft1(vr, vi, w1, tc, ts)
    fn = _fft2(yr, yi, w2, fscale)
    hf, hb = _mlstm(qvo, kt, *_mlstm_gates(gt, p["a_gbias"]))
    x = _a_out(x, fn, hf, hb, qvo, p["a_wout"])
    x = _ffn(x, p["f_norm"][0], p["f_wu"][0], p["f_cw"][0], p["f_cb"][0], p["f_wd"][0], p["final"], False)
    zw = SSD_GROUPS * SSD_HPG * SSD_P
    z, xbc_pre, cvg, dtt = _c_in(x, p["c_norm"], p["c_w"], p["c_wdt"], zw, zw + 2 * SSD_GROUPS * SSD_N, 2 * CONF_W)
    xbc = _ssd_conv(xbc_pre, p["c_scw"], p["c_scb"])
    yc = _conf(cvg, p["c_ccw"], p["c_ccb"], p["c_lng"], p["c_lnb"])
    yf, yb = _ssd(xbc, *_ssd_gates(dtt, p["c_dtbias"], p["c_aneg"]))
    x = _c_out(x, yf, yb, xbc, z, yc, p["c_dsk"], p["c_sng"], p["c_wout"])
    x = _ffn(x, p["f_norm"][1], p["f_wu"][1], p["f_cw"][1], p["f_cb"][1], p["f_wd"][1], p["final"], True)
    return x


def _prep(a_norm, a_w_in, a_gate_bias, a_w_out, c_norm, c_w_in, c_ssd_conv_w, c_ssd_conv_b, c_dt_bias, c_a_log,
          c_d_skip, c_ssd_norm, c_conf_conv_w, c_conf_conv_b, c_conf_ln_g, c_conf_ln_b, c_w_out,
          ffn_norm, ffn_w_up, ffn_conv_w, ffn_conv_b, ffn_w_down, final_norm):
    fw = FNET_GROUPS * FNET_DIM
    mw = MLSTM_HEADS * MLSTM_DIM
    zw = SSD_GROUPS * SSD_HPG * SSD_P
    xw = zw + 2 * SSD_GROUPS * SSD_N
    ndt = 2 * SSD_GROUPS * SSD_HPG
    wa = a_w_in[0]
    wc = c_w_in[0]
    row = lambda v: v.reshape(1, -1).astype(F32)
    return {
        "a_norm": row(a_norm[0]),
        "a_wf": wa[:, :fw].astype(BF16),
        "a_wq": jnp.concatenate([wa[:, fw:fw + mw], wa[:, fw + 2 * mw:fw + 4 * mw]], axis=1).astype(BF16),
        "a_wkt": wa[:, fw + mw:fw + 2 * mw].T.astype(BF16),
        "a_wgt": wa[:, fw + 4 * mw:].T.astype(BF16),
        "a_gbias": a_gate_bias[0].reshape(-1, 1).astype(F32),
        "a_wout": a_w_out[0].astype(BF16),
        "c_norm": row(c_norm[0]),
        "c_w": jnp.concatenate([wc[:, :zw + xw], wc[:, zw + xw + ndt:]], axis=1).astype(BF16),
        "c_wdt": wc[:, zw + xw:zw + xw + ndt].T.astype(BF16),
        "c_scw": c_ssd_conv_w[0].astype(F32),
        "c_scb": row(c_ssd_conv_b[0]),
        "c_dtbias": c_dt_bias[0].reshape(-1, 1).astype(F32),
        "c_aneg": (-jnp.exp(c_a_log[0].astype(F32))).reshape(-1, 1),
        "c_dsk": jnp.repeat(c_d_skip[0].reshape(-1).astype(F32), SSD_P).reshape(1, -1),
        "c_sng": row(c_ssd_norm[0]),
        "c_ccw": c_conf_conv_w[0].astype(F32),
        "c_ccb": row(c_conf_conv_b[0]),
        "c_lng": row(c_conf_ln_g[0]),
        "c_lnb": row(c_conf_ln_b[0]),
        "c_wout": c_w_out[0].astype(BF16),
        "f_norm": [row(ffn_norm[i]) for i in range(2)],
        "f_wu": [ffn_w_up[i].astype(BF16) for i in range(2)],
        "f_cw": [ffn_conv_w[i].astype(F32) for i in range(2)],
        "f_cb": [row(ffn_conv_b[i]) for i in range(2)],
        "f_wd": [ffn_w_down[i].astype(BF16) for i in range(2)],
        "final": row(final_norm),
    }


def kernel(x_prompt, x_sample, a_norm, a_w_in, a_gate_bias, a_w_out, c_norm, c_w_in, c_ssd_conv_w, c_ssd_conv_b, c_dt_bias, c_a_log, c_d_skip, c_ssd_norm, c_conf_conv_w, c_conf_conv_b, c_conf_ln_g, c_conf_ln_b, c_w_out, ffn_norm, ffn_w_up, ffn_conv_w, ffn_conv_b, ffn_w_down, final_norm):
    assert a_norm.shape[0] == 1 and c_norm.shape[0] == 1 and ffn_norm.shape[0] == 2
    p = _prep(a_norm, a_w_in, a_gate_bias, a_w_out, c_norm, c_w_in, c_ssd_conv_w, c_ssd_conv_b, c_dt_bias, c_a_log,
              c_d_skip, c_ssd_norm, c_conf_conv_w, c_conf_conv_b, c_conf_ln_g, c_conf_ln_b, c_w_out,
              ffn_norm, ffn_w_up, ffn_conv_w, ffn_conv_b, ffn_w_down, final_norm)
    outs = []
    consts = {}
    for x in (x_prompt, x_sample):
        s = x.shape[1]
        assert s % (LANES * SUBLANES) == 0
        if s not in consts:
            consts[s] = _dft_consts(s)
        outs.append(_encode(x.astype(F32), p, consts[s]))
    return tuple(outs)
```

```python
import functools
import math

import numpy as np
import jax
import jax.numpy as jnp
from jax import lax
from jax.experimental import pallas as pl
from jax.experimental.pallas import tpu as pltpu

F32 = jnp.float32
BF16 = jnp.bfloat16
EPS = 1e-6

LANES = 128
SUBLANES = 8
CHUNK = 128
VMEM_LIMIT = 56 * 1024 * 1024

FNET_GROUPS = 4
FNET_DIM = 128
MLSTM_HEADS = 4
MLSTM_DIM = 128
SSD_GROUPS = 4
SSD_HPG = 4
SSD_P = 64
SSD_N = 128
CONF_W = 512

TM_PROJ = 512
TM_FFN = 512
TM_CONV = 512
HALO = 8
HALO_CONF = 16
FFN_HC = 512

_NT = (((1,), (1,)), ((), ()))
_TN = (((0,), (0,)), ((), ()))


def _cparams(*sem):
    return pltpu.CompilerParams(dimension_semantics=sem, vmem_limit_bytes=VMEM_LIMIT)


def _const_spec(shape):
    nd = len(shape)
    return pl.BlockSpec(shape, lambda *_: (0,) * nd, pipeline_mode=pl.Buffered(1))


def _dot(a, b):
    return jnp.dot(a, b, preferred_element_type=F32)


def _rms(x, g):
    return x * lax.rsqrt(jnp.mean(x * x, axis=-1, keepdims=True) + EPS) * g


def _softplus(x):
    return jnp.maximum(x, 0.0) + jnp.log1p(jnp.exp(-jnp.abs(x)))


def _log_sigmoid(x):
    return -_softplus(-x)


def _silu(x):
    return x * jax.nn.sigmoid(x)


def _scan_lanes(x, combine, fill):
    ax = x.ndim - 1
    lane = lax.broadcasted_iota(jnp.int32, x.shape, ax)
    pre, suf = x, x
    k = 1
    while k < CHUNK:
        pre = combine(pre, jnp.where(lane >= k, pltpu.roll(pre, k, axis=ax), fill))
        suf = combine(suf, jnp.where(lane < CHUNK - k, pltpu.roll(suf, CHUNK - k, axis=ax), fill))
        k *= 2
    return pre, suf


def _cols_out(cols_ref, stk_scr):
    def body(c, carry):
        cols_ref[0, c] = stk_scr[c].T
        return carry
    lax.fori_loop(0, stk_scr.shape[0], body, 0)


def _pack2(hi, lo):
    hb = lax.bitcast_convert_type(hi.astype(BF16).astype(F32), jnp.uint32)
    lb = lax.bitcast_convert_type(lo.astype(BF16).astype(F32), jnp.uint32)
    return hb | (lb >> 16)


def _unpack2(w):
    hi = lax.bitcast_convert_type(w & jnp.uint32(0xFFFF0000), F32)
    lo = lax.bitcast_convert_type(w << 16, F32)
    return hi.astype(BF16), lo.astype(BF16)


def _chunk_specs(shape, nc):
    fwd = pl.BlockSpec((1, 1) + shape, lambda i, c: (i, c, 0, 0))
    bwd = pl.BlockSpec((1, 1) + shape, lambda i, c: (i, nc - 1 - c, 0, 0))
    return fwd, bwd


def _a_in_kernel(x_ref, g_ref, wf_ref, dft_ref, wq_ref, wkt_ref, wgt_ref, v_ref, qvo_ref, kt_ref, gt_ref):
    xb = _rms(x_ref[0], g_ref[...]).astype(BF16)
    hf = _dot(xb, wf_ref[...]).astype(BF16)
    for g in range(FNET_GROUPS):
        pq = _dot(hf[:, g * FNET_DIM:(g + 1) * FNET_DIM], dft_ref[...])
        v_ref[0, g] = _pack2(pq[:, :FNET_DIM], pq[:, FNET_DIM:])
    mw = MLSTM_HEADS * MLSTM_DIM
    qvo_ref[0, :, 0:mw] = (_dot(xb, wq_ref[:, 0:mw]) * (MLSTM_DIM ** -0.5)).astype(BF16)
    for j in range(1, qvo_ref.shape[2] // mw):
        sl = slice(j * mw, (j + 1) * mw)
        qvo_ref[0, :, sl] = _dot(xb, wq_ref[:, sl]).astype(BF16)
    kt_ref[0] = lax.dot_general(wkt_ref[...], xb, _NT, preferred_element_type=F32).astype(BF16)
    gt = lax.dot_general(wgt_ref[...], xb, _NT, preferred_element_type=F32)
    for c in range(gt_ref.shape[1]):
        gt_ref[0, c] = gt[:, c * CHUNK:(c + 1) * CHUNK]


def _a_in(x, g, wf, dft, wq, wkt, wgt):
    b, s, d = x.shape
    tm = min(TM_PROJ, s)
    fw = wf.shape[1]
    nq = wq.shape[1]
    kw = wkt.shape[0]
    ng = wgt.shape[0]
    return pl.pallas_call(
        _a_in_kernel,
        grid=(b, s // tm),
        in_specs=[
            pl.BlockSpec((1, tm, d), lambda i, m: (i, m, 0)),
            _const_spec((1, d)), _const_spec(wf.shape), _const_spec(dft.shape),
            _const_spec(wq.shape), _const_spec(wkt.shape), _const_spec(wgt.shape),
        ],
        out_specs=[
            pl.BlockSpec((1, FNET_GROUPS, tm, FNET_DIM), lambda i, m: (i, 0, m, 0)),
            pl.BlockSpec((1, tm, nq), lambda i, m: (i, m, 0)),
            pl.BlockSpec((1, kw, tm), lambda i, m: (i, 0, m)),
            pl.BlockSpec((1, tm // CHUNK, ng, CHUNK), lambda i, m: (i, m, 0, 0)),
        ],
        out_shape=[
            jax.ShapeDtypeStruct((b, FNET_GROUPS, s, FNET_DIM), jnp.uint32),
            jax.ShapeDtypeStruct((b, s, nq), BF16),
            jax.ShapeDtypeStruct((b, kw, s), BF16),
            jax.ShapeDtypeStruct((b, s // CHUNK, ng, CHUNK), F32),
        ],
        compiler_params=_cparams("parallel", "parallel"),
        name="a_in",
    )(x, g, wf, dft, wq, wkt, wgt)


def _fft1_kernel(v_ref, w1_ref, tc_ref, ts_ref, y_ref, vflat):
    ng, s1, ts2 = v_ref.shape[1:4]
    vflat[...] = v_ref[0].reshape(ng, s1 * ts2, LANES)
    for i in range(ts2):
        vr, vi = zip(*[_unpack2(vflat[g, pl.ds(i, s1, stride=ts2), :]) for g in range(ng)])
        v = jnp.concatenate([jnp.concatenate(vr, axis=1), jnp.concatenate(vi, axis=1)], axis=0)
        y = _dot(w1_ref[...], v)
        tc = tc_ref[i]
        ts = ts_ref[i]
        for g in range(ng):
            yr = y[:s1, g * LANES:(g + 1) * LANES]
            yi = y[s1:, g * LANES:(g + 1) * LANES]
            y_ref[0, g, i] = _pack2(yr * tc + yi * ts, yi * tc - yr * ts)


def _fft1(v, w1, tc, ts):
    b, ng, s, _ = v.shape
    s1 = s // LANES
    ts2 = SUBLANES
    v5 = v.reshape(b, ng, s1, LANES, LANES)
    spec_tw = pl.BlockSpec((ts2, s1, LANES), lambda j, i: (j, 0, 0))
    return pl.pallas_call(
        _fft1_kernel,
        grid=(LANES // ts2, b),
        in_specs=[pl.BlockSpec((1, ng, s1, ts2, LANES), lambda j, i: (i, 0, 0, j, 0)),
                  _const_spec(w1.shape), spec_tw, spec_tw],
        out_specs=pl.BlockSpec((1, ng, ts2, s1, LANES), lambda j, i: (i, 0, j, 0, 0)),
        out_shape=jax.ShapeDtypeStruct((b, ng, LANES, s1, LANES), jnp.uint32),
        scratch_shapes=[pltpu.VMEM((ng, s1 * ts2, LANES), jnp.uint32)],
        compiler_params=_cparams("parallel", "parallel"),
        name="fft1",
    )(v5, w1, tc, ts)


def _fft2_kernel(y_ref, w2_ref, o_ref, yflat, oflat, *, scale):
    ng, ns2, tk1 = y_ref.shape[1:4]
    yflat[...] = y_ref[0].reshape(ng, ns2 * tk1, LANES)
    for i in range(tk1):
        yr, yi = zip(*[_unpack2(yflat[g, pl.ds(i, ns2, stride=tk1), :]) for g in range(ng)])
        y = jnp.concatenate([jnp.concatenate(yr, axis=1), jnp.concatenate(yi, axis=1)], axis=0)
        out = _dot(w2_ref[...], y) * scale
        for g in range(ng):
            oflat[g, pl.ds(i, ns2, stride=tk1), :] = out[:, g * LANES:(g + 1) * LANES]
    o_ref[0] = oflat[...].reshape(ng, ns2, tk1, LANES)


def _fft2(y, w2, scale):
    b, ng, _, s1, _ = y.shape
    tk1 = SUBLANES
    spec = pl.BlockSpec((1, ng, LANES, tk1, LANES), lambda i, j: (i, 0, 0, j, 0))
    out = pl.pallas_call(
        functools.partial(_fft2_kernel, scale=scale),
        grid=(b, s1 // tk1),
        in_specs=[spec, _const_spec(w2.shape)],
        out_specs=spec,
        out_shape=jax.ShapeDtypeStruct((b, ng, LANES, s1, LANES), F32),
        scratch_shapes=[pltpu.VMEM((ng, LANES * tk1, LANES), jnp.uint32),
                        pltpu.VMEM((ng, LANES * tk1, LANES), F32)],
        compiler_params=_cparams("parallel", "parallel"),
        name="fft2",
    )(y, w2)
    return out.reshape(b, ng, LANES * s1, LANES)


def _mlstm_gates_kernel(gt_ref, bias_ref, rows_ref, cols_ref, tot_scr, amax_scr, mpf_scr, mpb_scr, stk_scr):
    nh = MLSTM_HEADS
    nu = 2 * nh
    nc = gt_ref.shape[1]
    g = gt_ref[0]
    bias = bias_ref[...][None]
    li = g[:, 0:nu] + bias[:, 0:nu]
    lf = _log_sigmoid(g[:, nu:] + bias[:, nu:])
    isf = lax.broadcasted_iota(jnp.int32, li.shape, 1) < nh
    pre, suf = _scan_lanes(lf, jnp.add, 0.0)
    bcs = jnp.where(isf, pre, suf)
    tot = pre + suf - lf
    e = li - bcs
    a = tot - bcs + li
    pa, sa = _scan_lanes(a, jnp.maximum, -jnp.inf)
    amax = jnp.maximum(pa, sa)
    pe, se = _scan_lanes(e, jnp.maximum, -jnp.inf)
    cmax = jnp.where(isf, pe, se)
    tot_scr[...] = tot
    amax_scr[...] = amax

    def fwd(c, m):
        mpf_scr[c] = m
        return jnp.maximum(tot_scr[c] + m, amax_scr[c])

    def bwd(i, m):
        c = nc - 1 - i
        mpb_scr[c] = m
        return jnp.maximum(tot_scr[c] + m, amax_scr[c])

    zero = jnp.zeros((nu, CHUNK), F32)
    lax.fori_loop(0, nc, fwd, zero)
    lax.fori_loop(0, nc, bwd, zero)
    m_prev = jnp.where(isf, mpf_scr[...], mpb_scr[...])
    m_new = jnp.maximum(tot + m_prev, amax)
    mu = jnp.maximum(m_prev, cmax)
    rows_ref[0, :, 0:nu] = e
    rows_ref[0, :, nu:2 * nu] = jnp.exp(a - m_new)
    rows_ref[0, :, 2 * nu:3 * nu] = jnp.exp(tot + m_prev - m_new)
    stk_scr[:, 0:nu] = mu
    stk_scr[:, nu:2 * nu] = jnp.exp(m_prev - mu)
    stk_scr[:, 2 * nu:3 * nu] = jnp.exp(-(bcs + mu))
    stk_scr[:, 3 * nu:] = jnp.zeros((nc, CHUNK - 3 * nu, CHUNK), F32)
    _cols_out(cols_ref, stk_scr)


def _mlstm_gates(gt, bias_col):
    b, nc, ng, _ = gt.shape
    nu = ng // 2
    small = pltpu.VMEM((nc, nu, CHUNK), F32)
    return pl.pallas_call(
        _mlstm_gates_kernel,
        grid=(b,),
        in_specs=[pl.BlockSpec((1, nc, ng, CHUNK), lambda i: (i, 0, 0, 0)), _const_spec(bias_col.shape)],
        out_specs=[pl.BlockSpec((1, nc, 3 * nu, CHUNK), lambda i: (i, 0, 0, 0)),
                   pl.BlockSpec((1, nc, CHUNK, CHUNK), lambda i: (i, 0, 0, 0))],
        out_shape=[jax.ShapeDtypeStruct((b, nc, 3 * nu, CHUNK), F32),
                   jax.ShapeDtypeStruct((b, nc, CHUNK, CHUNK), F32)],
        scratch_shapes=[small, small, small, small, pltpu.VMEM((nc, CHUNK, CHUNK), F32)],
        compiler_params=_cparams("parallel"),
        name="mlstm_gates",
    )(gt, bias_col)


def _mlstm_kernel(qf_ref, ktf_ref, vf_ref, qb_ref, ktb_ref, vb_ref, rf_ref, rb_ref, cf_ref, cb_ref,
                  hf_ref, hb_ref, cst_ref):
    nh, dh = MLSTM_HEADS, MLSTM_DIM
    nu = 2 * nh
    L = CHUNK

    @pl.when(pl.program_id(1) == 0)
    def _():
        cst_ref[...] = jnp.zeros_like(cst_ref)

    t_i = lax.broadcasted_iota(jnp.int32, (L, L), 0)
    s_i = lax.broadcasted_iota(jnp.int32, (L, L), 1)
    top = lax.broadcasted_iota(jnp.int32, (2 * dh, L), 0) < dh
    ones = jnp.ones((L, dh), BF16)
    for d in range(2):
        q_ref, kt_ref, v_ref, r_ref, c_ref, o_ref = ((qf_ref, ktf_ref, vf_ref, rf_ref, cf_ref, hf_ref) if d == 0 else
                                                     (qb_ref, ktb_ref, vb_ref, rb_ref, cb_ref, hb_ref))
        tri = (s_i <= t_i) if d == 0 else (s_i >= t_i)
        for hp in range(nh // 2):
            psl = slice(2 * hp * dh, (2 * hp + 2) * dh)
            kt2 = kt_ref[0, psl, :]
            zero = jnp.zeros_like(kt2)
            kbd = jnp.concatenate([jnp.where(top, kt2, zero), jnp.where(top, zero, kt2)], axis=1)
            sc2 = _dot(q_ref[0, :, psl], kbd)
            for hh in range(2):
                h = 2 * hp + hh
                u = d * nh + h
                sl = slice(h * dh, (h + 1) * dh)
                q = q_ref[0, :, sl]
                v = v_ref[0, :, sl]
                vaug = jnp.concatenate([v, ones], axis=1)
                e_r = r_ref[0, 0, u:u + 1, :]
                ea_r = r_ref[0, 0, nu + u:nu + u + 1, :]
                decay = r_ref[0, 0, 2 * nu + u:2 * nu + u + 1, 0:1]
                mu_c = c_ref[0, 0, :, u:u + 1]
                isc_c = c_ref[0, 0, :, nu + u:nu + u + 1]
                emt_c = c_ref[0, 0, :, 2 * nu + u:2 * nu + u + 1]
                w = jnp.exp(jnp.where(tri, e_r - mu_c, -jnp.inf))
                sw = (sc2[:, hh * L:(hh + 1) * L] * w).astype(BF16)
                qi = (q.astype(F32) * isc_c).astype(BF16)
                cs = cst_ref[u]
                tot = _dot(jnp.concatenate([sw, qi], axis=1),
                           jnp.concatenate([vaug, cs.astype(BF16)], axis=0))
                den = jnp.maximum(jnp.abs(tot[:, dh:]), emt_c)
                o_ref[0, :, sl] = (tot[:, :dh] / den).astype(o_ref.dtype)
                kw = (kt_ref[0, sl, :].astype(F32) * ea_r).astype(BF16)
                cst_ref[u] = decay * cs + _dot(kw, vaug)


def _mlstm(qvo, kt, rows, cols):
    b, s, _ = qvo.shape
    nc = s // CHUNK
    w = MLSTM_HEADS * MLSTM_DIM
    nr = rows.shape[2]

    def tok(j, rev):
        if rev:
            return pl.BlockSpec((1, CHUNK, w), lambda i, c: (i, nc - 1 - c, j))
        return pl.BlockSpec((1, CHUNK, w), lambda i, c: (i, c, j))

    ktf = pl.BlockSpec((1, w, CHUNK), lambda i, c: (i, 0, c))
    ktb = pl.BlockSpec((1, w, CHUNK), lambda i, c: (i, 0, nc - 1 - c))
    return pl.pallas_call(
        _mlstm_kernel,
        grid=(b, nc),
        in_specs=[tok(0, False), ktf, tok(1, False), tok(0, True), ktb, tok(1, True),
                  *_chunk_specs((nr, CHUNK), nc), *_chunk_specs((CHUNK, CHUNK), nc)],
        out_specs=[tok(0, False), tok(0, True)],
        out_shape=[jax.ShapeDtypeStruct((b, s, w), BF16)] * 2,
        scratch_shapes=[pltpu.VMEM((2 * MLSTM_HEADS, MLSTM_DIM, 2 * MLSTM_DIM), F32)],
        compiler_params=_cparams("parallel", "arbitrary"),
        name="mlstm",
    )(qvo, kt, qvo, qvo, kt, qvo, rows, rows, cols, cols)


def _a_out_kernel(x_ref, fn_ref, hf_ref, hb_ref, o_ref, w_ref, y_ref):
    ng, _, gd = fn_ref.shape[1:]
    mix = (jax.nn.sigmoid(o_ref[0].astype(F32)) * (hf_ref[0].astype(F32) + hb_ref[0].astype(F32))).astype(BF16)
    y = x_ref[0] + _dot(mix, w_ref[ng * gd:])
    for g in range(ng):
        y = y + _dot(fn_ref[0, g].astype(BF16), w_ref[g * gd:(g + 1) * gd])
    y_ref[0] = y


def _a_out(x, fn, hf, hb, qvo, w):
    b, s, d = x.shape
    tm = min(TM_PROJ, s)
    ng, gd = fn.shape[1], fn.shape[3]
    mw = hf.shape[2]
    tok = lambda width, j=0: pl.BlockSpec((1, tm, width), lambda i, m: (i, m, j))
    return pl.pallas_call(
        _a_out_kernel,
        grid=(b, s // tm),
        in_specs=[tok(d), pl.BlockSpec((1, ng, tm, gd), lambda i, m: (i, 0, m, 0)), tok(mw), tok(mw), tok(mw, 2),
                  _const_spec(w.shape)],
        out_specs=tok(d),
        out_shape=jax.ShapeDtypeStruct((b, s, d), F32),
        compiler_params=_cparams("parallel", "parallel"),
        name="a_out",
    )(x, fn, hf, hb, qvo, w)


def _halo_specs(tm, halo, width, s):
    r = tm // halo
    nb = s // halo
    mid = pl.BlockSpec((1, tm, width), lambda i, m: (i, m, 0))
    top = pl.BlockSpec((1, halo, width), lambda i, m: (i, jnp.maximum(m * r - 1, 0), 0))
    bot = pl.BlockSpec((1, halo, width), lambda i, m: (i, jnp.minimum((m + 1) * r, nb - 1), 0))
    return mid, top, bot


def _ffn_kernel(x_ref, top_ref, bot_ref, g_ref, wu_ref, cw_ref, cb_ref, wd_ref, fg_ref, y_ref, a_scr, *, final_norm):
    m = pl.program_id(1)
    tm = x_ref.shape[1]
    fh = wd_ref.shape[0]
    g = g_ref[...]
    x = x_ref[0]
    top = jnp.where(m > 0, _rms(top_ref[0], g), 0.0)
    bot = jnp.where(m < pl.num_programs(1) - 1, _rms(bot_ref[0], g), 0.0)
    xe = jnp.concatenate([top, _rms(x, g), bot], axis=0).astype(BF16)
    n_ext = tm + 2 * HALO

    def conv3(h, c0):
        w = cw_ref[:, c0:c0 + FFN_HC]
        out = (w[0:1] * pltpu.roll(h, 1, axis=0) + w[1:2] * h + w[2:3] * pltpu.roll(h, n_ext - 1, axis=0))
        return out[HALO:HALO + tm] + cb_ref[:, c0:c0 + FFN_HC]

    for j in range(fh // FFN_HC):
        c0 = j * FFN_HC
        gate = conv3(_dot(xe, wu_ref[:, c0:c0 + FFN_HC]), c0)
        val = conv3(_dot(xe, wu_ref[:, fh + c0:fh + c0 + FFN_HC]), fh + c0)
        a_scr[:, c0:c0 + FFN_HC] = (_silu(gate) * val).astype(BF16)
    y = x + _dot(a_scr[...], wd_ref[...])
    if final_norm:
        y = _rms(y, fg_ref[...])
    y_ref[0] = y


def _ffn(x, g, wu, cw, cb, wd, fg, final_norm):
    b, s, d = x.shape
    tm = min(TM_FFN, s)
    fh = wd.shape[0]
    mid, top, bot = _halo_specs(tm, HALO, d, s)
    return pl.pallas_call(
        functools.partial(_ffn_kernel, final_norm=final_norm),
        grid=(b, s // tm),
        in_specs=[mid, top, bot, _const_spec((1, d)), _const_spec(wu.shape), _const_spec(cw.shape),
                  _const_spec(cb.shape), _const_spec(wd.shape), _const_spec((1, d))],
        out_specs=mid,
        out_shape=jax.ShapeDtypeStruct((b, s, d), F32),
        scratch_shapes=[pltpu.VMEM((tm, fh), BF16)],
        compiler_params=_cparams("parallel", "parallel"),
        name="ffn",
    )(x, x, x, g, wu, cw, cb, wd, fg)


def _dwconv_taps(xe, w, tm, halo):
    k = w.shape[0]
    pad = (k - 1) // 2
    n = xe.shape[0]
    acc = None
    for j in range(k):
        off = halo - pad + j
        tap = w[j:j + 1] * pltpu.roll(xe, (n - off) % n, axis=0)[0:tm]
        acc = tap if acc is None else acc + tap
    return acc


def _c_in_kernel(x_ref, top_ref, bot_ref, g_ref, w_ref, wdt_ref, scw_ref, scb_ref, ccw_ref, ccb_ref, lg_ref, lb_ref,
                 z_ref, xbc_ref, yc_ref, dtt_ref):
    m = pl.program_id(1)
    tm = x_ref.shape[1]
    g = g_ref[...]
    zw = z_ref.shape[2]
    xw = xbc_ref.shape[2]
    cw = yc_ref.shape[2]
    ct = 512
    xm = _rms(x_ref[0], g)
    top = jnp.where(m > 0, _rms(top_ref[0], g), 0.0)
    bot = jnp.where(m < pl.num_programs(1) - 1, _rms(bot_ref[0], g), 0.0)
    xb = xm.astype(BF16)
    xe = jnp.concatenate([top, xm, bot], axis=0).astype(BF16)
    for j in range(zw // ct):
        sl = slice(j * ct, (j + 1) * ct)
        z_ref[0, :, sl] = _dot(xb, w_ref[:, sl]).astype(BF16)
    dtt = lax.dot_general(wdt_ref[...], xb, _NT, preferred_element_type=F32)
    for c in range(dtt_ref.shape[1]):
        dtt_ref[0, c] = dtt[:, c * CHUNK:(c + 1) * CHUNK]
    for j in range(xw // ct):
        sl = slice(j * ct, (j + 1) * ct)
        h = _dot(xe, w_ref[:, zw + j * ct:zw + (j + 1) * ct])
        xbc_ref[0, :, sl] = _silu(_dwconv_taps(h, scw_ref[:, sl], tm, HALO_CONF) + scb_ref[:, sl]).astype(BF16)
    val = _dot(xe, w_ref[:, zw + xw:zw + xw + cw])
    gate = _dot(xe, w_ref[:, zw + xw + cw:zw + xw + 2 * cw])
    u = _dwconv_taps(val * jax.nn.sigmoid(gate), ccw_ref[...], tm, HALO_CONF) + ccb_ref[...]
    mu = jnp.mean(u, axis=-1, keepdims=True)
    uc = u - mu
    var = jnp.mean(uc * uc, axis=-1, keepdims=True)
    yc_ref[0] = _silu(uc * lax.rsqrt(var + EPS) * lg_ref[...] + lb_ref[...]).astype(BF16)


def _c_in(x, g, w, wdt, scw, scb, ccw, ccb, lg, lb):
    b, s, d = x.shape
    tm = min(TM_PROJ, s)
    nd = wdt.shape[0]
    xw = scw.shape[1]
    cw = ccw.shape[1]
    zw = w.shape[1] - xw - 2 * cw
    mid, top, bot = _halo_specs(tm, HALO_CONF, d, s)
    tok = lambda width: pl.BlockSpec((1, tm, width), lambda i, m: (i, m, 0))
    consts = [g, w, wdt, scw, scb, ccw, ccb, lg, lb]
    return pl.pallas_call(
        _c_in_kernel,
        grid=(b, s // tm),
        in_specs=[mid, top, bot] + [_const_spec(a.shape) for a in consts],
        out_specs=[tok(zw), tok(xw), tok(cw),
                   pl.BlockSpec((1, tm // CHUNK, nd, CHUNK), lambda i, m: (i, m, 0, 0))],
        out_shape=[jax.ShapeDtypeStruct((b, s, zw), BF16), jax.ShapeDtypeStruct((b, s, xw), BF16),
                   jax.ShapeDtypeStruct((b, s, cw), BF16), jax.ShapeDtypeStruct((b, s // CHUNK, nd, CHUNK), F32)],
        compiler_params=_cparams("parallel", "parallel"),
        name="c_in",
    )(x, x, x, *consts)


def _ssd_gates_kernel(dt_ref, bias_ref, aneg_ref, rows_ref, cols_ref, stk_scr):
    nh = SSD_GROUPS * SSD_HPG
    nr = 2 * nh
    nc = dt_ref.shape[1]
    dt = _softplus(dt_ref[0] + bias_ref[...][None])
    da = dt * aneg_ref[...][None]
    isf = lax.broadcasted_iota(jnp.int32, dt.shape, 1) < nh
    pre, suf = _scan_lanes(da, jnp.add, 0.0)
    acs = jnp.where(isf, pre, suf)
    tot = pre + suf - da
    rows_ref[0, :, 0:nr] = dt
    rows_ref[0, :, nr:2 * nr] = acs
    rows_ref[0, :, 2 * nr:3 * nr] = jnp.exp(tot - acs) * dt
    rows_ref[0, :, 3 * nr:4 * nr] = jnp.exp(tot)
    stk_scr[:, 0:nr] = acs
    stk_scr[:, nr:] = jnp.zeros((nc, CHUNK - nr, CHUNK), F32)
    _cols_out(cols_ref, stk_scr)


def _ssd_gates(dtt, dtbias_col, aneg_col):
    b, nc, nr, _ = dtt.shape
    assert 4 * nr == CHUNK
    full = pl.BlockSpec((1, nc, CHUNK, CHUNK), lambda i: (i, 0, 0, 0))
    return pl.pallas_call(
        _ssd_gates_kernel,
        grid=(b,),
        in_specs=[pl.BlockSpec((1, nc, nr, CHUNK), lambda i: (i, 0, 0, 0)), _const_spec(dtbias_col.shape),
                  _const_spec(aneg_col.shape)],
        out_specs=[full, full],
        out_shape=[jax.ShapeDtypeStruct((b, nc, CHUNK, CHUNK), F32)] * 2,
        scratch_shapes=[pltpu.VMEM((nc, CHUNK, CHUNK), F32)],
        compiler_params=_cparams("parallel"),
        name="ssd_gates",
    )(dtt, dtbias_col, aneg_col)


def _ssd_kernel(xf_ref, bf_ref, cf_ref, xb_ref, bb_ref, cb_ref, rf_ref, rb_ref, kf_ref, kb_ref,
                yf_ref, yb_ref, hst_ref):
    ng, nj, hp, ns = SSD_GROUPS, SSD_HPG, SSD_P, SSD_N
    nh = ng * nj
    nr = 2 * nh
    gw = nj * hp
    L = CHUNK

    @pl.when(pl.program_id(1) == 0)
    def _():
        hst_ref[...] = jnp.zeros_like(hst_ref)

    t_i = lax.broadcasted_iota(jnp.int32, (L, L), 0)
    s_i = lax.broadcasted_iota(jnp.int32, (L, L), 1)
    hid = lax.broadcasted_iota(jnp.int32, (1, gw), 1) // hp
    hmask = [(hid == j).astype(F32).astype(BF16) for j in range(nj)]
    low = lax.broadcasted_iota(jnp.int32, (1, 2 * hp), 1) < hp

    def pair_lanes(vals):
        return jnp.concatenate([jnp.where(low, vals[2 * i], vals[2 * i + 1]) for i in range(nj // 2)], axis=1)

    for d in range(2):
        x_ref, b_ref, c_ref, r_ref, k_ref, y_ref = ((xf_ref, bf_ref, cf_ref, rf_ref, kf_ref, yf_ref) if d == 0 else
                                                    (xb_ref, bb_ref, cb_ref, rb_ref, kb_ref, yb_ref))
        tri = (s_i <= t_i) if d == 0 else (s_i >= t_i)
        for g in range(ng):
            u = d * ng + g
            rows = [d * nh + g * nj + j for j in range(nj)]
            x = x_ref[0, :, g * gw:(g + 1) * gw]
            bm = b_ref[0, :, g * ns:(g + 1) * ns]
            cm = c_ref[0, :, g * ns:(g + 1) * ns]
            cbm = lax.dot_general(cm, bm, _NT, preferred_element_type=F32)
            bmt = bm.T.astype(F32)
            yd = None
            st = None
            esc = []
            for i in range(nj // 2):
                ws, bws, xs = [], [], []
                for j in (2 * i, 2 * i + 1):
                    r = rows[j]
                    acs_c = jnp.broadcast_to(k_ref[0, 0, :, r:r + 1], (L, L))
                    dec = jnp.exp(jnp.where(tri, acs_c - r_ref[0, 0, nr + r:nr + r + 1, :], -jnp.inf))
                    ws.append((cbm * dec * r_ref[0, 0, r:r + 1, :]).astype(BF16))
                    bws.append((bmt * r_ref[0, 0, 2 * nr + r:2 * nr + r + 1, :]).astype(BF16))
                    xs.append(x * hmask[j])
                    esc.append(jnp.exp(acs_c))
                xk = jnp.concatenate(xs, axis=0)
                part = _dot(jnp.concatenate(ws, axis=1), xk)
                spart = _dot(jnp.concatenate(bws, axis=1), xk)
                yd = part if yd is None else yd + part
                st = spart if st is None else st + spart
            hprev = hst_ref[u]
            yoff = _dot(cm, hprev.astype(BF16))
            y_ref[0, :, g * gw:(g + 1) * gw] = (yd + yoff * pair_lanes(esc)).astype(y_ref.dtype)
            dsc = pair_lanes([r_ref[0, 0, 3 * nr + r:3 * nr + r + 1, :] for r in rows])
            hst_ref[u] = dsc * hprev + st


def _ssd(xbc, rows, cols):
    b, s, _ = xbc.shape
    nc = s // CHUNK
    xw = SSD_GROUPS * SSD_HPG * SSD_P
    bw = SSD_GROUPS * SSD_N
    assert xw == 2 * bw and 2 * SSD_P == LANES

    def spec(width, j, rev):
        if rev:
            return pl.BlockSpec((1, CHUNK, width), lambda i, c: (i, nc - 1 - c, j))
        return pl.BlockSpec((1, CHUNK, width), lambda i, c: (i, c, j))

    return pl.pallas_call(
        _ssd_kernel,
        grid=(b, nc),
        in_specs=[spec(xw, 0, False), spec(bw, 2, False), spec(bw, 3, False),
                  spec(xw, 0, True), spec(bw, 2, True), spec(bw, 3, True),
                  *_chunk_specs((CHUNK, CHUNK), nc), *_chunk_specs((CHUNK, CHUNK), nc)],
        out_specs=[spec(xw, 0, False), spec(xw, 0, True)],
        out_shape=[jax.ShapeDtypeStruct((b, s, xw), BF16)] * 2,
        scratch_shapes=[pltpu.VMEM((2 * SSD_GROUPS, SSD_N, SSD_HPG * SSD_P), F32)],
        compiler_params=_cparams("parallel", "arbitrary"),
        name="ssd",
    )(xbc, xbc, xbc, xbc, xbc, xbc, rows, rows, cols, cols)


def _c_out_kernel(x_ref, yf_ref, yb_ref, xs_ref, z_ref, yc_ref, dsk_ref, ng_ref, w_ref, o_ref):
    sw = yf_ref.shape[2]
    y = yf_ref[0].astype(F32) + yb_ref[0].astype(F32) + dsk_ref[...] * xs_ref[0].astype(F32)
    y = y * _silu(z_ref[0].astype(F32))
    yn = _rms(y, ng_ref[...]).astype(BF16)
    o_ref[0] = x_ref[0] + _dot(yn, w_ref[0:sw]) + _dot(yc_ref[0], w_ref[sw:])


def _c_out(x, yf, yb, xbc, z, yc, dsk, ngain, w):
    b, s, d = x.shape
    tm = min(TM_PROJ, s)
    sw = yf.shape[2]
    cw = yc.shape[2]
    tok = lambda width: pl.BlockSpec((1, tm, width), lambda i, m: (i, m, 0))
    return pl.pallas_call(
        _c_out_kernel,
        grid=(b, s // tm),
        in_specs=[tok(d), tok(sw), tok(sw), tok(sw), tok(sw), tok(cw), _const_spec((1, sw)),
                  _const_spec((1, sw)), _const_spec(w.shape)],
        out_specs=tok(d),
        out_shape=jax.ShapeDtypeStruct((b, s, d), F32),
        compiler_params=_cparams("parallel", "parallel"),
        name="c_out",
    )(x, yf, yb, xbc, z, yc, dsk, ngain, w)


def _dft_consts(s):
    s1 = s // LANES
    c = np.arange(FNET_DIM)
    ang = 2.0 * np.pi * np.outer(c, c) / FNET_DIM
    dft = np.concatenate([np.cos(ang), -np.sin(ang)], axis=1)
    k1 = np.arange(s1)
    a1 = 2.0 * np.pi * np.outer(k1, k1) / s1
    c1, sn1 = np.cos(a1), np.sin(a1)
    w1 = np.block([[c1, sn1], [-sn1, c1]])
    s2 = np.arange(LANES)
    at = 2.0 * np.pi * (s2[:, None] * k1[None, :]) / s
    tc = np.broadcast_to(np.cos(at)[:, :, None], (LANES, s1, LANES))
    ts = np.broadcast_to(np.sin(at)[:, :, None], (LANES, s1, LANES))
    a2 = 2.0 * np.pi * np.outer(s2, s2) / LANES
    w2 = np.concatenate([np.cos(a2), np.sin(a2)], axis=1)
    scale = 1.0 / math.sqrt(s * FNET_DIM)
    return (jnp.asarray(dft, BF16), jnp.asarray(w1, BF16), jnp.asarray(tc, F32), jnp.asarray(ts, F32),
            jnp.asarray(w2, BF16), scale)


def _encode(x, p, consts):
    dft, w1, tc, ts, w2, fscale = consts
    v, qvo, kt, gt = _a_in(x, p["a_norm"], p["a_wf"], dft, p["a_wq"], p["a_wkt"], p["a_wgt"])
    fn = _fft2(_fft1(v, w1, tc, ts), w2, fscale)
    hf, hb = _mlstm(qvo, kt, *_mlstm_gates(gt, p["a_gbias"]))
    x = _a_out(x, fn, hf, hb, qvo, p["a_wout"])
    x = _ffn(x, p["f_norm"][0], p["f_wu"][0], p["f_cw"][0], p["f_cb"][0], p["f_wd"][0], p["final"], False)
    z, xbc, yc, dtt = _c_in(x, p["c_norm"], p["c_w"], p["c_wdt"], p["c_scw"], p["c_scb"], p["c_ccw"], p["c_ccb"],
                            p["c_lng"], p["c_lnb"])
    yf, yb = _ssd(xbc, *_ssd_gates(dtt, p["c_dtbias"], p["c_aneg"]))
    x = _c_out(x, yf, yb, xbc, z, yc, p["c_dsk"], p["c_sng"], p["c_wout"])
    x = _ffn(x, p["f_norm"][1], p["f_wu"][1], p["f_cw"][1], p["f_cb"][1], p["f_wd"][1], p["final"], True)
    return x


def _prep(a_norm, a_w_in, a_gate_bias, a_w_out, c_norm, c_w_in, c_ssd_conv_w, c_ssd_conv_b, c_dt_bias, c_a_log,
          c_d_skip, c_ssd_norm, c_conf_conv_w, c_conf_conv_b, c_conf_ln_g, c_conf_ln_b, c_w_out,
          ffn_norm, ffn_w_up, ffn_conv_w, ffn_conv_b, ffn_w_down, final_norm):
    fw = FNET_GROUPS * FNET_DIM
    mw = MLSTM_HEADS * MLSTM_DIM
    zw = SSD_GROUPS * SSD_HPG * SSD_P
    xw = zw + 2 * SSD_GROUPS * SSD_N
    ndt = 2 * SSD_GROUPS * SSD_HPG
    wa = a_w_in[0]
    wc = c_w_in[0]
    row = lambda v: v.reshape(1, -1).astype(F32)
    return {
        "a_norm": row(a_norm[0]),
        "a_wf": wa[:, :fw].astype(BF16),
        "a_wq": jnp.concatenate([wa[:, fw:fw + mw], wa[:, fw + 2 * mw:fw + 4 * mw]], axis=1).astype(BF16),
        "a_wkt": wa[:, fw + mw:fw + 2 * mw].T.astype(BF16),
        "a_wgt": wa[:, fw + 4 * mw:].T.astype(BF16),
        "a_gbias": a_gate_bias[0].reshape(-1, 1).astype(F32),
        "a_wout": a_w_out[0].astype(BF16),
        "c_norm": row(c_norm[0]),
        "c_w": jnp.concatenate([wc[:, :zw + xw], wc[:, zw + xw + ndt:]], axis=1).astype(BF16),
        "c_wdt": wc[:, zw + xw:zw + xw + ndt].T.astype(BF16),
        "c_scw": c_ssd_conv_w[0].astype(F32),
        "c_scb": row(c_ssd_conv_b[0]),
        "c_dtbias": c_dt_bias[0].reshape(-1, 1).astype(F32),
        "c_aneg": (-jnp.exp(c_a_log[0].astype(F32))).reshape(-1, 1),
        "c_dsk": jnp.repeat(c_d_skip[0].reshape(-1).astype(F32), SSD_P).reshape(1, -1),
        "c_sng": row(c_ssd_norm[0]),
        "c_ccw": c_conf_conv_w[0].astype(F32),
        "c_ccb": row(c_conf_conv_b[0]),
        "c_lng": row(c_conf_ln_g[0]),
        "c_lnb": row(c_conf_ln_b[0]),
        "c_wout": c_w_out[0].astype(BF16),
        "f_norm": [row(ffn_norm[i]) for i in range(2)],
        "f_wu": [ffn_w_up[i].astype(BF16) for i in range(2)],
        "f_cw": [ffn_conv_w[i].astype(F32) for i in range(2)],
        "f_cb": [row(ffn_conv_b[i]) for i in range(2)],
        "f_wd": [ffn_w_down[i].astype(BF16) for i in range(2)],
        "final": row(final_norm),
    }


def kernel(x_prompt, x_sample, a_norm, a_w_in, a_gate_bias, a_w_out, c_norm, c_w_in, c_ssd_conv_w, c_ssd_conv_b, c_dt_bias, c_a_log, c_d_skip, c_ssd_norm, c_conf_conv_w, c_conf_conv_b, c_conf_ln_g, c_conf_ln_b, c_w_out, ffn_norm, ffn_w_up, ffn_conv_w, ffn_conv_b, ffn_w_down, final_norm):
    assert a_norm.shape[0] == 1 and c_norm.shape[0] == 1 and ffn_norm.shape[0] == 2
    p = _prep(a_norm, a_w_in, a_gate_bias, a_w_out, c_norm, c_w_in, c_ssd_conv_w, c_ssd_conv_b, c_dt_bias, c_a_log,
              c_d_skip, c_ssd_norm, c_conf_conv_w, c_conf_conv_b, c_conf_ln_g, c_conf_ln_b, c_w_out,
              ffn_norm, ffn_w_up, ffn_conv_w, ffn_conv_b, ffn_w_down, final_norm)
    outs = []
    consts = {}
    for x in (x_prompt, x_sample):
        s = x.shape[1]
        assert s % (LANES * SUBLANES) == 0
        if s not in consts:
            consts[s] = _dft_consts(s)
        outs.append(_encode(x.astype(F32), p, consts[s]))
    return tuple(outs)
```

```python
import functools
import math

import numpy as np
import jax
import jax.numpy as jnp
from jax import lax
from jax.experimental import pallas as pl
from jax.experimental.pallas import tpu as pltpu

F32 = jnp.float32
BF16 = jnp.bfloat16
EPS = 1e-6

LANES = 128
SUBLANES = 8
CHUNK = 128
VMEM_LIMIT = 56 * 1024 * 1024

FNET_GROUPS = 4
FNET_DIM = 128
MLSTM_HEADS = 4
MLSTM_DIM = 128
SSD_GROUPS = 4
SSD_HPG = 4
SSD_P = 64
SSD_N = 128
CONF_W = 512

TM_PROJ = 512
TM_FFN = 512
HALO_CONF = 16
HALO_T = 16
FFN_HC = 512

_NT = (((1,), (1,)), ((), ()))
_TN = (((0,), (0,)), ((), ()))


def _cparams(*sem):
    return pltpu.CompilerParams(dimension_semantics=sem, vmem_limit_bytes=VMEM_LIMIT)


def _const_spec(shape):
    nd = len(shape)
    return pl.BlockSpec(shape, lambda *_: (0,) * nd, pipeline_mode=pl.Buffered(1))


def _dot(a, b):
    return jnp.dot(a, b, preferred_element_type=F32)


def _rms(x, g):
    return x * lax.rsqrt(jnp.mean(x * x, axis=-1, keepdims=True) + EPS) * g


def _softplus(x):
    return jnp.maximum(x, 0.0) + jnp.log1p(jnp.exp(-jnp.abs(x)))


def _log_sigmoid(x):
    return -_softplus(-x)


def _silu(x):
    return x * jax.nn.sigmoid(x)


def _scan_lanes(x, combine, fill):
    ax = x.ndim - 1
    lane = lax.broadcasted_iota(jnp.int32, x.shape, ax)
    pre, suf = x, x
    k = 1
    while k < CHUNK:
        pre = combine(pre, jnp.where(lane >= k, pltpu.roll(pre, k, axis=ax), fill))
        suf = combine(suf, jnp.where(lane < CHUNK - k, pltpu.roll(suf, CHUNK - k, axis=ax), fill))
        k *= 2
    return pre, suf


def _cols_out(cols_ref, stk_scr):
    def body(c, carry):
        cols_ref[0, c] = stk_scr[c].T
        return carry
    lax.fori_loop(0, stk_scr.shape[0], body, 0)


def _pack2(hi, lo):
    hb = lax.bitcast_convert_type(hi.astype(BF16).astype(F32), jnp.uint32)
    lb = lax.bitcast_convert_type(lo.astype(BF16).astype(F32), jnp.uint32)
    return hb | (lb >> 16)


def _unpack2(w):
    hi = lax.bitcast_convert_type(w & jnp.uint32(0xFFFF0000), F32)
    lo = lax.bitcast_convert_type(w << 16, F32)
    return hi.astype(BF16), lo.astype(BF16)


def _chunk_specs(shape, nc):
    fwd = pl.BlockSpec((1, 1) + shape, lambda i, c: (i, c, 0, 0))
    bwd = pl.BlockSpec((1, 1) + shape, lambda i, c: (i, nc - 1 - c, 0, 0))
    return fwd, bwd


def _a_in_kernel(x_ref, g_ref, wf_ref, dft_ref, wq_ref, wkt_ref, wgt_ref, v_ref, qvo_ref, kt_ref, gt_ref):
    xb = _rms(x_ref[0], g_ref[...]).astype(BF16)
    hf = _dot(xb, wf_ref[...]).astype(BF16)
    for g in range(FNET_GROUPS):
        pq = _dot(hf[:, g * FNET_DIM:(g + 1) * FNET_DIM], dft_ref[...])
        v_ref[0, g] = _pack2(pq[:, :FNET_DIM], pq[:, FNET_DIM:])
    mw = MLSTM_HEADS * MLSTM_DIM
    qvo_ref[0, :, 0:mw] = (_dot(xb, wq_ref[:, 0:mw]) * (MLSTM_DIM ** -0.5)).astype(BF16)
    for j in range(1, qvo_ref.shape[2] // mw):
        sl = slice(j * mw, (j + 1) * mw)
        qvo_ref[0, :, sl] = _dot(xb, wq_ref[:, sl]).astype(BF16)
    kt_ref[0] = lax.dot_general(wkt_ref[...], xb, _NT, preferred_element_type=F32).astype(BF16)
    gt = lax.dot_general(wgt_ref[...], xb, _NT, preferred_element_type=F32)
    for c in range(gt_ref.shape[1]):
        gt_ref[0, c] = gt[:, c * CHUNK:(c + 1) * CHUNK]


def _a_in(x, g, wf, dft, wq, wkt, wgt):
    b, s, d = x.shape
    tm = min(TM_PROJ, s)
    fw = wf.shape[1]
    nq = wq.shape[1]
    kw = wkt.shape[0]
    ng = wgt.shape[0]
    return pl.pallas_call(
        _a_in_kernel,
        grid=(b, s // tm),
        in_specs=[
            pl.BlockSpec((1, tm, d), lambda i, m: (i, m, 0)),
            _const_spec((1, d)), _const_spec(wf.shape), _const_spec(dft.shape),
            _const_spec(wq.shape), _const_spec(wkt.shape), _const_spec(wgt.shape),
        ],
        out_specs=[
            pl.BlockSpec((1, FNET_GROUPS, tm, FNET_DIM), lambda i, m: (i, 0, m, 0)),
            pl.BlockSpec((1, tm, nq), lambda i, m: (i, m, 0)),
            pl.BlockSpec((1, kw, tm), lambda i, m: (i, 0, m)),
            pl.BlockSpec((1, tm // CHUNK, ng, CHUNK), lambda i, m: (i, m, 0, 0)),
        ],
        out_shape=[
            jax.ShapeDtypeStruct((b, FNET_GROUPS, s, FNET_DIM), jnp.uint32),
            jax.ShapeDtypeStruct((b, s, nq), BF16),
            jax.ShapeDtypeStruct((b, kw, s), BF16),
            jax.ShapeDtypeStruct((b, s // CHUNK, ng, CHUNK), F32),
        ],
        compiler_params=_cparams("parallel", "parallel"),
        name="a_in",
    )(x, g, wf, dft, wq, wkt, wgt)


def _fft1_kernel(v_ref, w1_ref, tc_ref, ts_ref, y_ref, vflat):
    ng, s1, ts2 = v_ref.shape[1:4]
    vflat[...] = v_ref[0].reshape(ng, s1 * ts2, LANES)
    for i in range(ts2):
        vr, vi = zip(*[_unpack2(vflat[g, pl.ds(i, s1, stride=ts2), :]) for g in range(ng)])
        v = jnp.concatenate([jnp.concatenate(vr, axis=1), jnp.concatenate(vi, axis=1)], axis=0)
        y = _dot(w1_ref[...], v)
        tc = tc_ref[i]
        ts = ts_ref[i]
        for g in range(ng):
            yr = y[:s1, g * LANES:(g + 1) * LANES]
            yi = y[s1:, g * LANES:(g + 1) * LANES]
            y_ref[0, g, i] = _pack2(yr * tc + yi * ts, yi * tc - yr * ts)


def _fft1(v, w1, tc, ts):
    b, ng, s, _ = v.shape
    s1 = s // LANES
    ts2 = SUBLANES
    v5 = v.reshape(b, ng, s1, LANES, LANES)
    spec_tw = pl.BlockSpec((ts2, s1, LANES), lambda j, i: (j, 0, 0))
    return pl.pallas_call(
        _fft1_kernel,
        grid=(LANES // ts2, b),
        in_specs=[pl.BlockSpec((1, ng, s1, ts2, LANES), lambda j, i: (i, 0, 0, j, 0)),
                  _const_spec(w1.shape), spec_tw, spec_tw],
        out_specs=pl.BlockSpec((1, ng, ts2, s1, LANES), lambda j, i: (i, 0, j, 0, 0)),
        out_shape=jax.ShapeDtypeStruct((b, ng, LANES, s1, LANES), jnp.uint32),
        scratch_shapes=[pltpu.VMEM((ng, s1 * ts2, LANES), jnp.uint32)],
        compiler_params=_cparams("parallel", "parallel"),
        name="fft1",
    )(v5, w1, tc, ts)


def _fft2_kernel(y_ref, w2_ref, o_ref, yflat, oflat, *, scale):
    ng, ns2, tk1 = y_ref.shape[1:4]
    yflat[...] = y_ref[0].reshape(ng, ns2 * tk1, LANES)
    for i in range(tk1):
        yr, yi = zip(*[_unpack2(yflat[g, pl.ds(i, ns2, stride=tk1), :]) for g in range(ng)])
        y = jnp.concatenate([jnp.concatenate(yr, axis=1), jnp.concatenate(yi, axis=1)], axis=0)
        out = _dot(w2_ref[...], y) * scale
        for g in range(ng):
            oflat[g, pl.ds(i, ns2, stride=tk1), :] = out[:, g * LANES:(g + 1) * LANES]
    o_ref[0] = oflat[...].reshape(ng, ns2, tk1, LANES)


def _fft2(y, w2, scale):
    b, ng, _, s1, _ = y.shape
    tk1 = SUBLANES
    spec = pl.BlockSpec((1, ng, LANES, tk1, LANES), lambda i, j: (i, 0, 0, j, 0))
    out = pl.pallas_call(
        functools.partial(_fft2_kernel, scale=scale),
        grid=(b, s1 // tk1),
        in_specs=[spec, _const_spec(w2.shape)],
        out_specs=spec,
        out_shape=jax.ShapeDtypeStruct((b, ng, LANES, s1, LANES), F32),
        scratch_shapes=[pltpu.VMEM((ng, LANES * tk1, LANES), jnp.uint32),
                        pltpu.VMEM((ng, LANES * tk1, LANES), F32)],
        compiler_params=_cparams("parallel", "parallel"),
        name="fft2",
    )(y, w2)
    return out.reshape(b, ng, LANES * s1, LANES)


def _mlstm_gates_kernel(gt_ref, bias_ref, rows_ref, cols_ref, tot_scr, amax_scr, mpf_scr, mpb_scr, stk_scr):
    nh = MLSTM_HEADS
    nu = 2 * nh
    nc = gt_ref.shape[1]
    g = gt_ref[0]
    bias = bias_ref[...][None]
    li = g[:, 0:nu] + bias[:, 0:nu]
    lf = _log_sigmoid(g[:, nu:] + bias[:, nu:])
    isf = lax.broadcasted_iota(jnp.int32, li.shape, 1) < nh
    pre, suf = _scan_lanes(lf, jnp.add, 0.0)
    bcs = jnp.where(isf, pre, suf)
    tot = pre + suf - lf
    e = li - bcs
    a = tot - bcs + li
    pa, sa = _scan_lanes(a, jnp.maximum, -jnp.inf)
    amax = jnp.maximum(pa, sa)
    pe, se = _scan_lanes(e, jnp.maximum, -jnp.inf)
    cmax = jnp.where(isf, pe, se)
    tot_scr[...] = tot
    amax_scr[...] = amax

    def fwd(c, m):
        mpf_scr[c] = m
        return jnp.maximum(tot_scr[c] + m, amax_scr[c])

    def bwd(i, m):
        c = nc - 1 - i
        mpb_scr[c] = m
        return jnp.maximum(tot_scr[c] + m, amax_scr[c])

    zero = jnp.zeros((nu, CHUNK), F32)
    lax.fori_loop(0, nc, fwd, zero)
    lax.fori_loop(0, nc, bwd, zero)
    m_prev = jnp.where(isf, mpf_scr[...], mpb_scr[...])
    m_new = jnp.maximum(tot + m_prev, amax)
    mu = jnp.maximum(m_prev, cmax)
    rows_ref[0, :, 0:nu] = e
    rows_ref[0, :, nu:2 * nu] = jnp.exp(a - m_new)
    rows_ref[0, :, 2 * nu:3 * nu] = jnp.exp(tot + m_prev - m_new)
    stk_scr[:, 0:nu] = mu
    stk_scr[:, nu:2 * nu] = jnp.exp(m_prev - mu)
    stk_scr[:, 2 * nu:3 * nu] = jnp.exp(-(bcs + mu))
    stk_scr[:, 3 * nu:] = jnp.zeros((nc, CHUNK - 3 * nu, CHUNK), F32)
    _cols_out(cols_ref, stk_scr)


def _mlstm_gates(gt, bias_col):
    b, nc, ng, _ = gt.shape
    nu = ng // 2
    small = pltpu.VMEM((nc, nu, CHUNK), F32)
    return pl.pallas_call(
        _mlstm_gates_kernel,
        grid=(b,),
        in_specs=[pl.BlockSpec((1, nc, ng, CHUNK), lambda i: (i, 0, 0, 0)), _const_spec(bias_col.shape)],
        out_specs=[pl.BlockSpec((1, nc, 3 * nu, CHUNK), lambda i: (i, 0, 0, 0)),
                   pl.BlockSpec((1, nc, CHUNK, CHUNK), lambda i: (i, 0, 0, 0))],
        out_shape=[jax.ShapeDtypeStruct((b, nc, 3 * nu, CHUNK), F32),
                   jax.ShapeDtypeStruct((b, nc, CHUNK, CHUNK), F32)],
        scratch_shapes=[small, small, small, small, pltpu.VMEM((nc, CHUNK, CHUNK), F32)],
        compiler_params=_cparams("parallel"),
        name="mlstm_gates",
    )(gt, bias_col)


def _mlstm_kernel(qf_ref, ktf_ref, vf_ref, qb_ref, ktb_ref, vb_ref, rf_ref, rb_ref, cf_ref, cb_ref,
                  hf_ref, hb_ref, cst_ref):
    nh, dh = MLSTM_HEADS, MLSTM_DIM
    nu = 2 * nh
    L = CHUNK

    @pl.when(pl.program_id(1) == 0)
    def _():
        cst_ref[...] = jnp.zeros_like(cst_ref)

    t_i = lax.broadcasted_iota(jnp.int32, (L, L), 0)
    s_i = lax.broadcasted_iota(jnp.int32, (L, L), 1)
    top = lax.broadcasted_iota(jnp.int32, (2 * dh, L), 0) < dh
    ones = jnp.ones((L, dh), BF16)
    for d in range(2):
        q_ref, kt_ref, v_ref, r_ref, c_ref, o_ref = ((qf_ref, ktf_ref, vf_ref, rf_ref, cf_ref, hf_ref) if d == 0 else
                                                     (qb_ref, ktb_ref, vb_ref, rb_ref, cb_ref, hb_ref))
        tri = (s_i <= t_i) if d == 0 else (s_i >= t_i)
        for hp in range(nh // 2):
            psl = slice(2 * hp * dh, (2 * hp + 2) * dh)
            kt2 = kt_ref[0, psl, :]
            zero = jnp.zeros_like(kt2)
            kbd = jnp.concatenate([jnp.where(top, kt2, zero), jnp.where(top, zero, kt2)], axis=1)
            sc2 = _dot(q_ref[0, :, psl], kbd)
            for hh in range(2):
                h = 2 * hp + hh
                u = d * nh + h
                sl = slice(h * dh, (h + 1) * dh)
                q = q_ref[0, :, sl]
                v = v_ref[0, :, sl]
                vaug = jnp.concatenate([v, ones], axis=1)
                e_r = r_ref[0, 0, u:u + 1, :]
                ea_r = r_ref[0, 0, nu + u:nu + u + 1, :]
                decay = r_ref[0, 0, 2 * nu + u:2 * nu + u + 1, 0:1]
                mu_c = c_ref[0, 0, :, u:u + 1]
                isc_c = c_ref[0, 0, :, nu + u:nu + u + 1]
                emt_c = c_ref[0, 0, :, 2 * nu + u:2 * nu + u + 1]
                w = jnp.exp(jnp.where(tri, e_r - mu_c, -jnp.inf))
                sw = (sc2[:, hh * L:(hh + 1) * L] * w).astype(BF16)
                qi = (q.astype(F32) * isc_c).astype(BF16)
                cs = cst_ref[u]
                tot = _dot(jnp.concatenate([sw, qi], axis=1),
                           jnp.concatenate([vaug, cs.astype(BF16)], axis=0))
                den = jnp.maximum(jnp.abs(tot[:, dh:]), emt_c)
                o_ref[0, :, sl] = (tot[:, :dh] / den).astype(o_ref.dtype)
                kw = (kt_ref[0, sl, :].astype(F32) * ea_r).astype(BF16)
                cst_ref[u] = decay * cs + _dot(kw, vaug)


def _mlstm(qvo, kt, rows, cols):
    b, s, _ = qvo.shape
    nc = s // CHUNK
    w = MLSTM_HEADS * MLSTM_DIM
    nr = rows.shape[2]

    def tok(j, rev):
        if rev:
            return pl.BlockSpec((1, CHUNK, w), lambda i, c: (i, nc - 1 - c, j))
        return pl.BlockSpec((1, CHUNK, w), lambda i, c: (i, c, j))

    ktf = pl.BlockSpec((1, w, CHUNK), lambda i, c: (i, 0, c))
    ktb = pl.BlockSpec((1, w, CHUNK), lambda i, c: (i, 0, nc - 1 - c))
    return pl.pallas_call(
        _mlstm_kernel,
        grid=(b, nc),
        in_specs=[tok(0, False), ktf, tok(1, False), tok(0, True), ktb, tok(1, True),
                  *_chunk_specs((nr, CHUNK), nc), *_chunk_specs((CHUNK, CHUNK), nc)],
        out_specs=[tok(0, False), tok(0, True)],
        out_shape=[jax.ShapeDtypeStruct((b, s, w), BF16)] * 2,
        scratch_shapes=[pltpu.VMEM((2 * MLSTM_HEADS, MLSTM_DIM, 2 * MLSTM_DIM), F32)],
        compiler_params=_cparams("parallel", "arbitrary"),
        name="mlstm",
    )(qvo, kt, qvo, qvo, kt, qvo, rows, rows, cols, cols)


def _halo_specs(tm, halo, width, s, col=0, groups=None):
    r = tm // halo
    nb = s // halo
    rows = [(tm, lambda m: m), (halo, lambda m: jnp.maximum(m * r - 1, 0)),
            (halo, lambda m: jnp.minimum((m + 1) * r, nb - 1))]
    if groups is None:
        return [pl.BlockSpec((1, n, width), lambda i, m, f=f: (i, f(m), col)) for n, f in rows]
    return [pl.BlockSpec((1, groups, n, width), lambda i, m, f=f: (i, 0, f(m), 0)) for n, f in rows]


def _ext(refs, g=None):
    mid, top, bot = refs
    if g is None:
        return jnp.concatenate([top[0], mid[0], bot[0]], axis=0)
    return jnp.concatenate([top[0, g], mid[0, g], bot[0, g]], axis=0)


def _ffn_body(xa, g_ref, wu_ref, cw_ref, cb_ref, wd_ref, fg_ref, y_ref, a_scr, final_norm):
    m = pl.program_id(1)
    tm = y_ref.shape[1]
    fh = wd_ref.shape[0]
    n_ext = xa.shape[0]
    row = lax.broadcasted_iota(jnp.int32, (n_ext, 1), 0)
    first = jnp.where(m > 0, 0, HALO_T)
    last = jnp.where(m < pl.num_programs(1) - 1, n_ext, HALO_T + tm)
    xn = _rms(xa, g_ref[...])
    xe = jnp.where(row >= first, jnp.where(row < last, xn, 0.0), 0.0).astype(BF16)

    def conv3(h, c0):
        w = cw_ref[:, c0:c0 + FFN_HC]
        out = (w[0:1] * pltpu.roll(h, 1, axis=0) + w[1:2] * h + w[2:3] * pltpu.roll(h, n_ext - 1, axis=0))
        return out[HALO_T:HALO_T + tm] + cb_ref[:, c0:c0 + FFN_HC]

    for j in range(fh // FFN_HC):
        c0 = j * FFN_HC
        gate = conv3(_dot(xe, wu_ref[:, c0:c0 + FFN_HC]), c0)
        val = conv3(_dot(xe, wu_ref[:, fh + c0:fh + c0 + FFN_HC]), fh + c0)
        a_scr[:, c0:c0 + FFN_HC] = (_silu(gate) * val).astype(BF16)
    y = xa[HALO_T:HALO_T + tm] + _dot(a_scr[...], wd_ref[...])
    if final_norm:
        y = _rms(y, fg_ref[...])
    y_ref[0] = y


def _a_tail_kernel(*refs, final_norm):
    x, fn, hf, hb, o = (refs[3 * i:3 * i + 3] for i in range(5))
    wo_ref, g_ref, wu_ref, cw_ref, cb_ref, wd_ref, fg_ref, y_ref, a_scr = refs[15:]
    ng, _, gd = fn[0].shape[1:]
    h = _ext(hf).astype(F32) + _ext(hb).astype(F32)
    xa = _ext(x) + _dot((jax.nn.sigmoid(_ext(o).astype(F32)) * h).astype(BF16), wo_ref[ng * gd:])
    for g in range(ng):
        xa = xa + _dot(_ext(fn, g).astype(BF16), wo_ref[g * gd:(g + 1) * gd])
    _ffn_body(xa, g_ref, wu_ref, cw_ref, cb_ref, wd_ref, fg_ref, y_ref, a_scr, final_norm)


def _c_tail_kernel(*refs, final_norm):
    x, yf, yb, xs, z, yc = (refs[3 * i:3 * i + 3] for i in range(6))
    dsk_ref, sng_ref, wo_ref, g_ref, wu_ref, cw_ref, cb_ref, wd_ref, fg_ref, y_ref, a_scr = refs[18:]
    sw = yf[0].shape[2]
    y = _ext(yf).astype(F32) + _ext(yb).astype(F32) + dsk_ref[...] * _ext(xs).astype(F32)
    y = y * _silu(_ext(z).astype(F32))
    xa = _ext(x) + _dot(_rms(y, sng_ref[...]).astype(BF16), wo_ref[0:sw]) + _dot(_ext(yc), wo_ref[sw:])
    _ffn_body(xa, g_ref, wu_ref, cw_ref, cb_ref, wd_ref, fg_ref, y_ref, a_scr, final_norm)


def _tail_call(kernel, name, x, tok_specs, tok_args, consts, wd, final_norm):
    b, s, d = x.shape
    tm = min(TM_FFN, s)
    return pl.pallas_call(
        functools.partial(kernel, final_norm=final_norm),
        grid=(b, s // tm),
        in_specs=tok_specs + [_const_spec(a.shape) for a in consts],
        out_specs=pl.BlockSpec((1, tm, d), lambda i, m: (i, m, 0)),
        out_shape=jax.ShapeDtypeStruct((b, s, d), F32),
        scratch_shapes=[pltpu.VMEM((tm, wd.shape[0]), BF16)],
        compiler_params=_cparams("parallel", "parallel"),
        name=name,
    )(*tok_args, *consts)


def _a_tail(x, fn, hf, hb, qvo, wo, ffn, final_norm):
    b, s, d = x.shape
    tm = min(TM_FFN, s)
    mw = hf.shape[2]
    hs = functools.partial(_halo_specs, tm, HALO_T)
    specs = (hs(d, s) + hs(fn.shape[3], s, groups=fn.shape[1]) + hs(mw, s) + hs(mw, s) + hs(mw, s, col=2))
    args = [a for a in (x, fn, hf, hb, qvo) for _ in range(3)]
    return _tail_call(_a_tail_kernel, "a_tail", x, specs, args, [wo] + ffn, ffn[4], final_norm)


def _c_tail(x, yf, yb, xbc, z, yc, dsk, sng, wo, ffn, final_norm):
    b, s, d = x.shape
    tm = min(TM_FFN, s)
    sw = yf.shape[2]
    hs = functools.partial(_halo_specs, tm, HALO_T)
    specs = hs(d, s) + hs(sw, s) + hs(sw, s) + hs(sw, s) + hs(sw, s) + hs(yc.shape[2], s)
    args = [a for a in (x, yf, yb, xbc, z, yc) for _ in range(3)]
    return _tail_call(_c_tail_kernel, "c_tail", x, specs, args, [dsk, sng, wo] + ffn, ffn[4], final_norm)


def _dwconv_taps(xe, w, tm, halo):
    k = w.shape[0]
    pad = (k - 1) // 2
    n = xe.shape[0]
    acc = None
    for j in range(k):
        off = halo - pad + j
        tap = w[j:j + 1] * pltpu.roll(xe, (n - off) % n, axis=0)[0:tm]
        acc = tap if acc is None else acc + tap
    return acc


def _c_in_kernel(x_ref, top_ref, bot_ref, g_ref, w_ref, wdt_ref, scw_ref, scb_ref, ccw_ref, ccb_ref, lg_ref, lb_ref,
                 z_ref, xbc_ref, yc_ref, dtt_ref):
    m = pl.program_id(1)
    tm = x_ref.shape[1]
    g = g_ref[...]
    zw = z_ref.shape[2]
    xw = xbc_ref.shape[2]
    cw = yc_ref.shape[2]
    ct = 512
    xm = _rms(x_ref[0], g)
    top = jnp.where(m > 0, _rms(top_ref[0], g), 0.0)
    bot = jnp.where(m < pl.num_programs(1) - 1, _rms(bot_ref[0], g), 0.0)
    xb = xm.astype(BF16)
    xe = jnp.concatenate([top, xm, bot], axis=0).astype(BF16)
    for j in range(zw // ct):
        sl = slice(j * ct, (j + 1) * ct)
        z_ref[0, :, sl] = _dot(xb, w_ref[:, sl]).astype(BF16)
    dtt = lax.dot_general(wdt_ref[...], xb, _NT, preferred_element_type=F32)
    for c in range(dtt_ref.shape[1]):
        dtt_ref[0, c] = dtt[:, c * CHUNK:(c + 1) * CHUNK]
    for j in range(xw // ct):
        sl = slice(j * ct, (j + 1) * ct)
        h = _dot(xe, w_ref[:, zw + j * ct:zw + (j + 1) * ct])
        xbc_ref[0, :, sl] = _silu(_dwconv_taps(h, scw_ref[:, sl], tm, HALO_CONF) + scb_ref[:, sl]).astype(BF16)
    val = _dot(xe, w_ref[:, zw + xw:zw + xw + cw])
    gate = _dot(xe, w_ref[:, zw + xw + cw:zw + xw + 2 * cw])
    u = _dwconv_taps(val * jax.nn.sigmoid(gate), ccw_ref[...], tm, HALO_CONF) + ccb_ref[...]
    mu = jnp.mean(u, axis=-1, keepdims=True)
    uc = u - mu
    var = jnp.mean(uc * uc, axis=-1, keepdims=True)
    yc_ref[0] = _silu(uc * lax.rsqrt(var + EPS) * lg_ref[...] + lb_ref[...]).astype(BF16)


def _c_in(x, g, w, wdt, scw, scb, ccw, ccb, lg, lb):
    b, s, d = x.shape
    tm = min(TM_PROJ, s)
    nd = wdt.shape[0]
    xw = scw.shape[1]
    cw = ccw.shape[1]
    zw = w.shape[1] - xw - 2 * cw
    mid, top, bot = _halo_specs(tm, HALO_CONF, d, s)
    tok = lambda width: pl.BlockSpec((1, tm, width), lambda i, m: (i, m, 0))
    consts = [g, w, wdt, scw, scb, ccw, ccb, lg, lb]
    return pl.pallas_call(
        _c_in_kernel,
        grid=(b, s // tm),
        in_specs=[mid, top, bot] + [_const_spec(a.shape) for a in consts],
        out_specs=[tok(zw), tok(xw), tok(cw),
                   pl.BlockSpec((1, tm // CHUNK, nd, CHUNK), lambda i, m: (i, m, 0, 0))],
        out_shape=[jax.ShapeDtypeStruct((b, s, zw), BF16), jax.ShapeDtypeStruct((b, s, xw), BF16),
                   jax.ShapeDtypeStruct((b, s, cw), BF16), jax.ShapeDtypeStruct((b, s // CHUNK, nd, CHUNK), F32)],
        compiler_params=_cparams("parallel", "parallel"),
        name="c_in",
    )(x, x, x, *consts)


def _ssd_gates_kernel(dt_ref, bias_ref, aneg_ref, rows_ref, cols_ref, stk_scr):
    nh = SSD_GROUPS * SSD_HPG
    nr = 2 * nh
    nc = dt_ref.shape[1]
    dt = _softplus(dt_ref[0] + bias_ref[...][None])
    da = dt * aneg_ref[...][None]
    isf = lax.broadcasted_iota(jnp.int32, dt.shape, 1) < nh
    pre, suf = _scan_lanes(da, jnp.add, 0.0)
    acs = jnp.where(isf, pre, suf)
    tot = pre + suf - da
    rows_ref[0, :, 0:nr] = dt
    rows_ref[0, :, nr:2 * nr] = acs
    rows_ref[0, :, 2 * nr:3 * nr] = jnp.exp(tot - acs) * dt
    rows_ref[0, :, 3 * nr:4 * nr] = jnp.exp(tot)
    stk_scr[:, 0:nr] = acs
    stk_scr[:, nr:] = jnp.zeros((nc, CHUNK - nr, CHUNK), F32)
    _cols_out(cols_ref, stk_scr)


def _ssd_gates(dtt, dtbias_col, aneg_col):
    b, nc, nr, _ = dtt.shape
    assert 4 * nr == CHUNK
    full = pl.BlockSpec((1, nc, CHUNK, CHUNK), lambda i: (i, 0, 0, 0))
    return pl.pallas_call(
        _ssd_gates_kernel,
        grid=(b,),
        in_specs=[pl.BlockSpec((1, nc, nr, CHUNK), lambda i: (i, 0, 0, 0)), _const_spec(dtbias_col.shape),
                  _const_spec(aneg_col.shape)],
        out_specs=[full, full],
        out_shape=[jax.ShapeDtypeStruct((b, nc, CHUNK, CHUNK), F32)] * 2,
        scratch_shapes=[pltpu.VMEM((nc, CHUNK, CHUNK), F32)],
        compiler_params=_cparams("parallel"),
        name="ssd_gates",
    )(dtt, dtbias_col, aneg_col)


def _ssd_kernel(xf_ref, bf_ref, cf_ref, xb_ref, bb_ref, cb_ref, rf_ref, rb_ref, kf_ref, kb_ref,
                yf_ref, yb_ref, hst_ref):
    ng, nj, hp, ns = SSD_GROUPS, SSD_HPG, SSD_P, SSD_N
    nh = ng * nj
    nr = 2 * nh
    gw = nj * hp
    L = CHUNK

    @pl.when(pl.program_id(1) == 0)
    def _():
        hst_ref[...] = jnp.zeros_like(hst_ref)

    t_i = lax.broadcasted_iota(jnp.int32, (L, L), 0)
    s_i = lax.broadcasted_iota(jnp.int32, (L, L), 1)
    hid = lax.broadcasted_iota(jnp.int32, (1, gw), 1) // hp
    hmask = [(hid == j).astype(F32).astype(BF16) for j in range(nj)]
    low = lax.broadcasted_iota(jnp.int32, (1, 2 * hp), 1) < hp

    def pair_lanes(vals):
        return jnp.concatenate([jnp.where(low, vals[2 * i], vals[2 * i + 1]) for i in range(nj // 2)], axis=1)

    for d in range(2):
        x_ref, b_ref, c_ref, r_ref, k_ref, y_ref = ((xf_ref, bf_ref, cf_ref, rf_ref, kf_ref, yf_ref) if d == 0 else
                                                    (xb_ref, bb_ref, cb_ref, rb_ref, kb_ref, yb_ref))
        tri = (s_i <= t_i) if d == 0 else (s_i >= t_i)
        for g in range(ng):
            u = d * ng + g
            rows = [d * nh + g * nj + j for j in range(nj)]
            x = x_ref[0, :, g * gw:(g + 1) * gw]
            bm = b_ref[0, :, g * ns:(g + 1) * ns]
            cm = c_ref[0, :, g * ns:(g + 1) * ns]
            cbm = lax.dot_general(cm, bm, _NT, preferred_element_type=F32)
            bmt = bm.T.astype(F32)
            yd = None
            st = None
            esc = []
            for i in range(nj // 2):
                ws, bws, xs = [], [], []
                for j in (2 * i, 2 * i + 1):
                    r = rows[j]
                    acs_c = jnp.broadcast_to(k_ref[0, 0, :, r:r + 1], (L, L))
                    dec = jnp.exp(jnp.where(tri, acs_c - r_ref[0, 0, nr + r:nr + r + 1, :], -jnp.inf))
                    ws.append((cbm * dec * r_ref[0, 0, r:r + 1, :]).astype(BF16))
                    bws.append((bmt * r_ref[0, 0, 2 * nr + r:2 * nr + r + 1, :]).astype(BF16))
                    xs.append(x * hmask[j])
                    esc.append(jnp.exp(acs_c))
                xk = jnp.concatenate(xs, axis=0)
                part = _dot(jnp.concatenate(ws, axis=1), xk)
                spart = _dot(jnp.concatenate(bws, axis=1), xk)
                yd = part if yd is None else yd + part
                st = spart if st is None else st + spart
            hprev = hst_ref[u]
            yoff = _dot(cm, hprev.astype(BF16))
            y_ref[0, :, g * gw:(g + 1) * gw] = (yd + yoff * pair_lanes(esc)).astype(y_ref.dtype)
            dsc = pair_lanes([r_ref[0, 0, 3 * nr + r:3 * nr + r + 1, :] for r in rows])
            hst_ref[u] = dsc * hprev + st


def _ssd(xbc, rows, cols):
    b, s, _ = xbc.shape
    nc = s // CHUNK
    xw = SSD_GROUPS * SSD_HPG * SSD_P
    bw = SSD_GROUPS * SSD_N
    assert xw == 2 * bw and 2 * SSD_P == LANES

    def spec(width, j, rev):
        if rev:
            return pl.BlockSpec((1, CHUNK, width), lambda i, c: (i, nc - 1 - c, j))
        return pl.BlockSpec((1, CHUNK, width), lambda i, c: (i, c, j))

    return pl.pallas_call(
        _ssd_kernel,
        grid=(b, nc),
        in_specs=[spec(xw, 0, False), spec(bw, 2, False), spec(bw, 3, False),
                  spec(xw, 0, True), spec(bw, 2, True), spec(bw, 3, True),
                  *_chunk_specs((CHUNK, CHUNK), nc), *_chunk_specs((CHUNK, CHUNK), nc)],
        out_specs=[spec(xw, 0, False), spec(xw, 0, True)],
        out_shape=[jax.ShapeDtypeStruct((b, s, xw), BF16)] * 2,
        scratch_shapes=[pltpu.VMEM((2 * SSD_GROUPS, SSD_N, SSD_HPG * SSD_P), F32)],
        compiler_params=_cparams("parallel", "arbitrary"),
        name="ssd",
    )(xbc, xbc, xbc, xbc, xbc, xbc, rows, rows, cols, cols)


def _dft_consts(s):
    s1 = s // LANES
    c = np.arange(FNET_DIM)
    ang = 2.0 * np.pi * np.outer(c, c) / FNET_DIM
    dft = np.concatenate([np.cos(ang), -np.sin(ang)], axis=1)
    k1 = np.arange(s1)
    a1 = 2.0 * np.pi * np.outer(k1, k1) / s1
    c1, sn1 = np.cos(a1), np.sin(a1)
    w1 = np.block([[c1, sn1], [-sn1, c1]])
    s2 = np.arange(LANES)
    at = 2.0 * np.pi * (s2[:, None] * k1[None, :]) / s
    tc = np.broadcast_to(np.cos(at)[:, :, None], (LANES, s1, LANES))
    ts = np.broadcast_to(np.sin(at)[:, :, None], (LANES, s1, LANES))
    a2 = 2.0 * np.pi * np.outer(s2, s2) / LANES
    w2 = np.concatenate([np.cos(a2), np.sin(a2)], axis=1)
    scale = 1.0 / math.sqrt(s * FNET_DIM)
    return (jnp.asarray(dft, BF16), jnp.asarray(w1, BF16), jnp.asarray(tc, F32), jnp.asarray(ts, F32),
            jnp.asarray(w2, BF16), scale)


def _encode(x, p, consts):
    dft, w1, tc, ts, w2, fscale = consts
    v, qvo, kt, gt = _a_in(x, p["a_norm"], p["a_wf"], dft, p["a_wq"], p["a_wkt"], p["a_wgt"])
    fn = _fft2(_fft1(v, w1, tc, ts), w2, fscale)
    hf, hb = _mlstm(qvo, kt, *_mlstm_gates(gt, p["a_gbias"]))
    ffn = lambda i: [p["f_norm"][i], p["f_wu"][i], p["f_cw"][i], p["f_cb"][i], p["f_wd"][i], p["final"]]
    x = _a_tail(x, fn, hf, hb, qvo, p["a_wout"], ffn(0), False)
    z, xbc, yc, dtt = _c_in(x, p["c_norm"], p["c_w"], p["c_wdt"], p["c_scw"], p["c_scb"], p["c_ccw"], p["c_ccb"],
                            p["c_lng"], p["c_lnb"])
    yf, yb = _ssd(xbc, *_ssd_gates(dtt, p["c_dtbias"], p["c_aneg"]))
    return _c_tail(x, yf, yb, xbc, z, yc, p["c_dsk"], p["c_sng"], p["c_wout"], ffn(1), True)


def _prep(a_norm, a_w_in, a_gate_bias, a_w_out, c_norm, c_w_in, c_ssd_conv_w, c_ssd_conv_b, c_dt_bias, c_a_log,
          c_d_skip, c_ssd_norm, c_conf_conv_w, c_conf_conv_b, c_conf_ln_g, c_conf_ln_b, c_w_out,
          ffn_norm, ffn_w_up, ffn_conv_w, ffn_conv_b, ffn_w_down, final_norm):
    fw = FNET_GROUPS * FNET_DIM
    mw = MLSTM_HEADS * MLSTM_DIM
    zw = SSD_GROUPS * SSD_HPG * SSD_P
    xw = zw + 2 * SSD_GROUPS * SSD_N
    ndt = 2 * SSD_GROUPS * SSD_HPG
    wa = a_w_in[0]
    wc = c_w_in[0]
    row = lambda v: v.reshape(1, -1).astype(F32)
    return {
        "a_norm": row(a_norm[0]),
        "a_wf": wa[:, :fw].astype(BF16),
        "a_wq": jnp.concatenate([wa[:, fw:fw + mw], wa[:, fw + 2 * mw:fw + 4 * mw]], axis=1).astype(BF16),
        "a_wkt": wa[:, fw + mw:fw + 2 * mw].T.astype(BF16),
        "a_wgt": wa[:, fw + 4 * mw:].T.astype(BF16),
        "a_gbias": a_gate_bias[0].reshape(-1, 1).astype(F32),
        "a_wout": a_w_out[0].astype(BF16),
        "c_norm": row(c_norm[0]),
        "c_w": jnp.concatenate([wc[:, :zw + xw], wc[:, zw + xw + ndt:]], axis=1).astype(BF16),
        "c_wdt": wc[:, zw + xw:zw + xw + ndt].T.astype(BF16),
        "c_scw": c_ssd_conv_w[0].astype(F32),
        "c_scb": row(c_ssd_conv_b[0]),
        "c_dtbias": c_dt_bias[0].reshape(-1, 1).astype(F32),
        "c_aneg": (-jnp.exp(c_a_log[0].astype(F32))).reshape(-1, 1),
        "c_dsk": jnp.repeat(c_d_skip[0].reshape(-1).astype(F32), SSD_P).reshape(1, -1),
        "c_sng": row(c_ssd_norm[0]),
        "c_ccw": c_conf_conv_w[0].astype(F32),
        "c_ccb": row(c_conf_conv_b[0]),
        "c_lng": row(c_conf_ln_g[0]),
        "c_lnb": row(c_conf_ln_b[0]),
        "c_wout": c_w_out[0].astype(BF16),
        "f_norm": [row(ffn_norm[i]) for i in range(2)],
        "f_wu": [ffn_w_up[i].astype(BF16) for i in range(2)],
        "f_cw": [ffn_conv_w[i].astype(F32) for i in range(2)],
        "f_cb": [row(ffn_conv_b[i]) for i in range(2)],
        "f_wd": [ffn_w_down[i].astype(BF16) for i in range(2)],
        "final": row(final_norm),
    }


def kernel(x_prompt, x_sample, a_norm, a_w_in, a_gate_bias, a_w_out, c_norm, c_w_in, c_ssd_conv_w, c_ssd_conv_b, c_dt_bias, c_a_log, c_d_skip, c_ssd_norm, c_conf_conv_w, c_conf_conv_b, c_conf_ln_g, c_conf_ln_b, c_w_out, ffn_norm, ffn_w_up, ffn_conv_w, ffn_conv_b, ffn_w_down, final_norm):
    assert a_norm.shape[0] == 1 and c_norm.shape[0] == 1 and ffn_norm.shape[0] == 2
    p = _prep(a_norm, a_w_in, a_gate_bias, a_w_out, c_norm, c_w_in, c_ssd_conv_w, c_ssd_conv_b, c_dt_bias, c_a_log,
              c_d_skip, c_ssd_norm, c_conf_conv_w, c_conf_conv_b, c_conf_ln_g, c_conf_ln_b, c_w_out,
              ffn_norm, ffn_w_up, ffn_conv_w, ffn_conv_b, ffn_w_down, final_norm)
    outs = []
    consts = {}
    for x in (x_prompt, x_sample):
        s = x.shape[1]
        assert s % (LANES * SUBLANES) == 0
        if s not in consts:
            consts[s] = _dft_consts(s)
        outs.append(_encode(x.astype(F32), p, consts[s]))
    return tuple(outs)
```

```python
import functools
import math

import numpy as np
import jax
import jax.numpy as jnp
from jax import lax
from jax.experimental import pallas as pl
from jax.experimental.pallas import tpu as pltpu

F32 = jnp.float32
BF16 = jnp.bfloat16
EPS = 1e-6

LANES = 128
SUBLANES = 8
CHUNK = 128
VMEM_LIMIT = 56 * 1024 * 1024

FNET_GROUPS = 4
FNET_DIM = 128
MLSTM_HEADS = 4
MLSTM_DIM = 128
SSD_GROUPS = 4
SSD_HPG = 4
SSD_P = 64
SSD_N = 128
CONF_W = 512

TM_PROJ = 512
TM_FFN = 512
HALO_CONF = 16
HALO_T = 16
EXT = 8
CPS = 2
FFN_HC = 512

_NT = (((1,), (1,)), ((), ()))
_TN = (((0,), (0,)), ((), ()))


def _cparams(*sem):
    return pltpu.CompilerParams(dimension_semantics=sem, vmem_limit_bytes=VMEM_LIMIT)


def _const_spec(shape):
    nd = len(shape)
    return pl.BlockSpec(shape, lambda *_: (0,) * nd, pipeline_mode=pl.Buffered(1))


def _dot(a, b):
    return jnp.dot(a, b, preferred_element_type=F32)


def _rms(x, g):
    return x * lax.rsqrt(jnp.mean(x * x, axis=-1, keepdims=True) + EPS) * g


def _softplus(x):
    return jnp.maximum(x, 0.0) + jnp.log1p(jnp.exp(-jnp.abs(x)))


def _log_sigmoid(x):
    return -_softplus(-x)


def _silu(x):
    return x * jax.nn.sigmoid(x)


def _scan_lanes(x, combine, fill):
    ax = x.ndim - 1
    lane = lax.broadcasted_iota(jnp.int32, x.shape, ax)
    pre, suf = x, x
    k = 1
    while k < CHUNK:
        pre = combine(pre, jnp.where(lane >= k, pltpu.roll(pre, k, axis=ax), fill))
        suf = combine(suf, jnp.where(lane < CHUNK - k, pltpu.roll(suf, CHUNK - k, axis=ax), fill))
        k *= 2
    return pre, suf


def _cols_out(cols_ref, stk_scr):
    def body(c, carry):
        cols_ref[0, c] = stk_scr[c].T
        return carry
    lax.fori_loop(0, stk_scr.shape[0], body, 0)


def _pack2(hi, lo):
    hb = lax.bitcast_convert_type(hi.astype(BF16).astype(F32), jnp.uint32)
    lb = lax.bitcast_convert_type(lo.astype(BF16).astype(F32), jnp.uint32)
    return hb | (lb >> 16)


def _unpack2(w):
    hi = lax.bitcast_convert_type(w & jnp.uint32(0xFFFF0000), F32)
    lo = lax.bitcast_convert_type(w << 16, F32)
    return hi.astype(BF16), lo.astype(BF16)


def _chunk_specs(shape, nblk):
    fwd = pl.BlockSpec((1, CPS) + shape, lambda i, c: (i, c, 0, 0))
    bwd = pl.BlockSpec((1, CPS) + shape, lambda i, c: (i, nblk - 1 - c, 0, 0))
    return fwd, bwd


def _a_in_kernel(x_ref, g_ref, wf_ref, dft_ref, wq_ref, wkt_ref, wgt_ref, v_ref, qvo_ref, kt_ref, gt_ref):
    xb = _rms(x_ref[0], g_ref[...]).astype(BF16)
    hf = _dot(xb, wf_ref[...]).astype(BF16)
    for g in range(FNET_GROUPS):
        pq = _dot(hf[:, g * FNET_DIM:(g + 1) * FNET_DIM], dft_ref[...])
        v_ref[0, g] = _pack2(pq[:, :FNET_DIM], pq[:, FNET_DIM:])
    mw = MLSTM_HEADS * MLSTM_DIM
    qvo_ref[0, :, 0:mw] = (_dot(xb, wq_ref[:, 0:mw]) * (MLSTM_DIM ** -0.5)).astype(BF16)
    for j in range(1, qvo_ref.shape[2] // mw):
        sl = slice(j * mw, (j + 1) * mw)
        qvo_ref[0, :, sl] = _dot(xb, wq_ref[:, sl]).astype(BF16)
    kt_ref[0] = lax.dot_general(wkt_ref[...], xb, _NT, preferred_element_type=F32).astype(BF16)
    gt = lax.dot_general(wgt_ref[...], xb, _NT, preferred_element_type=F32)
    for c in range(gt_ref.shape[1]):
        gt_ref[0, c] = gt[:, c * CHUNK:(c + 1) * CHUNK]


def _a_in(x, g, wf, dft, wq, wkt, wgt):
    b, s, d = x.shape
    tm = min(TM_PROJ, s)
    fw = wf.shape[1]
    nq = wq.shape[1]
    kw = wkt.shape[0]
    ng = wgt.shape[0]
    return pl.pallas_call(
        _a_in_kernel,
        grid=(b, s // tm),
        in_specs=[
            pl.BlockSpec((1, tm, d), lambda i, m: (i, m, 0)),
            _const_spec((1, d)), _const_spec(wf.shape), _const_spec(dft.shape),
            _const_spec(wq.shape), _const_spec(wkt.shape), _const_spec(wgt.shape),
        ],
        out_specs=[
            pl.BlockSpec((1, FNET_GROUPS, tm, FNET_DIM), lambda i, m: (i, 0, m, 0)),
            pl.BlockSpec((1, tm, nq), lambda i, m: (i, m, 0)),
            pl.BlockSpec((1, kw, tm), lambda i, m: (i, 0, m)),
            pl.BlockSpec((1, tm // CHUNK, ng, CHUNK), lambda i, m: (i, m, 0, 0)),
        ],
        out_shape=[
            jax.ShapeDtypeStruct((b, FNET_GROUPS, s, FNET_DIM), jnp.uint32),
            jax.ShapeDtypeStruct((b, s, nq), BF16),
            jax.ShapeDtypeStruct((b, kw, s), BF16),
            jax.ShapeDtypeStruct((b, s // CHUNK, ng, CHUNK), F32),
        ],
        compiler_params=_cparams("parallel", "parallel"),
        name="a_in",
    )(x, g, wf, dft, wq, wkt, wgt)


def _fft1_kernel(v_ref, w1_ref, tc_ref, ts_ref, y_ref, vflat):
    ng, s1, ts2 = v_ref.shape[1:4]
    vflat[...] = v_ref[0].reshape(ng, s1 * ts2, LANES)
    for i in range(ts2):
        vr, vi = zip(*[_unpack2(vflat[g, pl.ds(i, s1, stride=ts2), :]) for g in range(ng)])
        v = jnp.concatenate([jnp.concatenate(vr, axis=1), jnp.concatenate(vi, axis=1)], axis=0)
        y = _dot(w1_ref[...], v)
        tc = tc_ref[i]
        ts = ts_ref[i]
        for g in range(ng):
            yr = y[:s1, g * LANES:(g + 1) * LANES]
            yi = y[s1:, g * LANES:(g + 1) * LANES]
            y_ref[0, g, i] = _pack2(yr * tc + yi * ts, yi * tc - yr * ts)


def _fft1(v, w1, tc, ts):
    b, ng, s, _ = v.shape
    s1 = s // LANES
    ts2 = SUBLANES
    v5 = v.reshape(b, ng, s1, LANES, LANES)
    spec_tw = pl.BlockSpec((ts2, s1, LANES), lambda j, i: (j, 0, 0))
    return pl.pallas_call(
        _fft1_kernel,
        grid=(LANES // ts2, b),
        in_specs=[pl.BlockSpec((1, ng, s1, ts2, LANES), lambda j, i: (i, 0, 0, j, 0)),
                  _const_spec(w1.shape), spec_tw, spec_tw],
        out_specs=pl.BlockSpec((1, ng, ts2, s1, LANES), lambda j, i: (i, 0, j, 0, 0)),
        out_shape=jax.ShapeDtypeStruct((b, ng, LANES, s1, LANES), jnp.uint32),
        scratch_shapes=[pltpu.VMEM((ng, s1 * ts2, LANES), jnp.uint32)],
        compiler_params=_cparams("parallel", "parallel"),
        name="fft1",
    )(v5, w1, tc, ts)


def _fft2_kernel(y_ref, w2_ref, o_ref, yflat, oflat, *, scale):
    ng, ns2, tk1 = y_ref.shape[1:4]
    yflat[...] = y_ref[0].reshape(ng, ns2 * tk1, LANES)
    for i in range(tk1):
        yr, yi = zip(*[_unpack2(yflat[g, pl.ds(i, ns2, stride=tk1), :]) for g in range(ng)])
        y = jnp.concatenate([jnp.concatenate(yr, axis=1), jnp.concatenate(yi, axis=1)], axis=0)
        out = _dot(w2_ref[...], y) * scale
        for g in range(ng):
            oflat[g, pl.ds(i, ns2, stride=tk1), :] = out[:, g * LANES:(g + 1) * LANES]
    o_ref[0] = oflat[...].reshape(ng, ns2, tk1, LANES)


def _fft2(y, w2, scale):
    b, ng, _, s1, _ = y.shape
    tk1 = SUBLANES
    spec = pl.BlockSpec((1, ng, LANES, tk1, LANES), lambda i, j: (i, 0, 0, j, 0))
    out = pl.pallas_call(
        functools.partial(_fft2_kernel, scale=scale),
        grid=(b, s1 // tk1),
        in_specs=[spec, _const_spec(w2.shape)],
        out_specs=spec,
        out_shape=jax.ShapeDtypeStruct((b, ng, LANES, s1, LANES), F32),
        scratch_shapes=[pltpu.VMEM((ng, LANES * tk1, LANES), jnp.uint32),
                        pltpu.VMEM((ng, LANES * tk1, LANES), F32)],
        compiler_params=_cparams("parallel", "parallel"),
        name="fft2",
    )(y, w2)
    return out.reshape(b, ng, LANES * s1, LANES)


def _mlstm_gates_kernel(gt_ref, bias_ref, rows_ref, cols_ref, tot_scr, amax_scr, mpf_scr, mpb_scr, stk_scr):
    nh = MLSTM_HEADS
    nu = 2 * nh
    nc = gt_ref.shape[1]
    g = gt_ref[0]
    bias = bias_ref[...][None]
    li = g[:, 0:nu] + bias[:, 0:nu]
    lf = _log_sigmoid(g[:, nu:] + bias[:, nu:])
    isf = lax.broadcasted_iota(jnp.int32, li.shape, 1) < nh
    pre, suf = _scan_lanes(lf, jnp.add, 0.0)
    bcs = jnp.where(isf, pre, suf)
    tot = pre + suf - lf
    e = li - bcs
    a = tot - bcs + li
    pa, sa = _scan_lanes(a, jnp.maximum, -jnp.inf)
    amax = jnp.maximum(pa, sa)
    pe, se = _scan_lanes(e, jnp.maximum, -jnp.inf)
    cmax = jnp.where(isf, pe, se)
    tot_scr[...] = tot
    amax_scr[...] = amax

    def fwd(c, m):
        mpf_scr[c] = m
        return jnp.maximum(tot_scr[c] + m, amax_scr[c])

    def bwd(i, m):
        c = nc - 1 - i
        mpb_scr[c] = m
        return jnp.maximum(tot_scr[c] + m, amax_scr[c])

    zero = jnp.zeros((nu, CHUNK), F32)
    lax.fori_loop(0, nc, fwd, zero)
    lax.fori_loop(0, nc, bwd, zero)
    m_prev = jnp.where(isf, mpf_scr[...], mpb_scr[...])
    m_new = jnp.maximum(tot + m_prev, amax)
    mu = jnp.maximum(m_prev, cmax)
    rows_ref[0, :, 0:nu] = e
    rows_ref[0, :, nu:2 * nu] = jnp.exp(a - m_new)
    rows_ref[0, :, 2 * nu:3 * nu] = jnp.exp(tot + m_prev - m_new)
    stk_scr[:, 0:nu] = mu
    stk_scr[:, nu:2 * nu] = jnp.exp(m_prev - mu)
    stk_scr[:, 2 * nu:3 * nu] = jnp.exp(-(bcs + mu))
    stk_scr[:, 3 * nu:] = jnp.zeros((nc, CHUNK - 3 * nu, CHUNK), F32)
    _cols_out(cols_ref, stk_scr)


def _mlstm_gates(gt, bias_col):
    b, nc, ng, _ = gt.shape
    nu = ng // 2
    small = pltpu.VMEM((nc, nu, CHUNK), F32)
    return pl.pallas_call(
        _mlstm_gates_kernel,
        grid=(b,),
        in_specs=[pl.BlockSpec((1, nc, ng, CHUNK), lambda i: (i, 0, 0, 0)), _const_spec(bias_col.shape)],
        out_specs=[pl.BlockSpec((1, nc, 3 * nu, CHUNK), lambda i: (i, 0, 0, 0)),
                   pl.BlockSpec((1, nc, CHUNK, CHUNK), lambda i: (i, 0, 0, 0))],
        out_shape=[jax.ShapeDtypeStruct((b, nc, 3 * nu, CHUNK), F32),
                   jax.ShapeDtypeStruct((b, nc, CHUNK, CHUNK), F32)],
        scratch_shapes=[small, small, small, small, pltpu.VMEM((nc, CHUNK, CHUNK), F32)],
        compiler_params=_cparams("parallel"),
        name="mlstm_gates",
    )(gt, bias_col)


def _mlstm_kernel(qf_ref, ktf_ref, vf_ref, qb_ref, ktb_ref, vb_ref, rf_ref, rb_ref, cf_ref, cb_ref,
                  hf_ref, hb_ref, cst_ref):
    nh, dh = MLSTM_HEADS, MLSTM_DIM
    nu = 2 * nh
    L = CHUNK

    @pl.when(pl.program_id(1) == 0)
    def _():
        cst_ref[...] = jnp.zeros_like(cst_ref)

    t_i = lax.broadcasted_iota(jnp.int32, (L, L), 0)
    s_i = lax.broadcasted_iota(jnp.int32, (L, L), 1)
    top = lax.broadcasted_iota(jnp.int32, (2 * dh, L), 0) < dh
    ones = jnp.ones((L, dh), BF16)
    for d in range(2):
        q_ref, kt_ref, v_ref, r_ref, c_ref, o_ref = ((qf_ref, ktf_ref, vf_ref, rf_ref, cf_ref, hf_ref) if d == 0 else
                                                     (qb_ref, ktb_ref, vb_ref, rb_ref, cb_ref, hb_ref))
        tri = (s_i <= t_i) if d == 0 else (s_i >= t_i)
        for step in range(CPS):
            ci = step if d == 0 else CPS - 1 - step
            tsl = slice(ci * L, (ci + 1) * L)
            for hp in range(nh // 2):
                psl = slice(2 * hp * dh, (2 * hp + 2) * dh)
                kt2 = kt_ref[0, psl, tsl]
                zero = jnp.zeros_like(kt2)
                kbd = jnp.concatenate([jnp.where(top, kt2, zero), jnp.where(top, zero, kt2)], axis=1)
                sc2 = _dot(q_ref[0, tsl, psl], kbd)
                for hh in range(2):
                    h = 2 * hp + hh
                    u = d * nh + h
                    sl = slice(h * dh, (h + 1) * dh)
                    q = q_ref[0, tsl, sl]
                    v = v_ref[0, tsl, sl]
                    vaug = jnp.concatenate([v, ones], axis=1)
                    e_r = r_ref[0, ci, u:u + 1, :]
                    ea_r = r_ref[0, ci, nu + u:nu + u + 1, :]
                    decay = r_ref[0, ci, 2 * nu + u:2 * nu + u + 1, 0:1]
                    mu_c = c_ref[0, ci, :, u:u + 1]
                    isc_c = c_ref[0, ci, :, nu + u:nu + u + 1]
                    emt_c = c_ref[0, ci, :, 2 * nu + u:2 * nu + u + 1]
                    w = jnp.exp(jnp.where(tri, e_r - mu_c, -jnp.inf))
                    sw = (sc2[:, hh * L:(hh + 1) * L] * w).astype(BF16)
                    qi = (q.astype(F32) * isc_c).astype(BF16)
                    cs = cst_ref[u]
                    tot = _dot(jnp.concatenate([sw, qi], axis=1),
                               jnp.concatenate([vaug, cs.astype(BF16)], axis=0))
                    den = jnp.maximum(jnp.abs(tot[:, dh:]), emt_c)
                    o_ref[0, tsl, sl] = (tot[:, :dh] / den).astype(o_ref.dtype)
                    kw = (kt_ref[0, sl, tsl].astype(F32) * ea_r).astype(BF16)
                    cst_ref[u] = decay * cs + _dot(kw, vaug)


def _mlstm(qvo, kt, rows, cols):
    b, s, _ = qvo.shape
    tl = CPS * CHUNK
    nblk = s // tl
    w = MLSTM_HEADS * MLSTM_DIM
    nr = rows.shape[2]

    def tok(j, rev):
        if rev:
            return pl.BlockSpec((1, tl, w), lambda i, c: (i, nblk - 1 - c, j))
        return pl.BlockSpec((1, tl, w), lambda i, c: (i, c, j))

    ktf = pl.BlockSpec((1, w, tl), lambda i, c: (i, 0, c))
    ktb = pl.BlockSpec((1, w, tl), lambda i, c: (i, 0, nblk - 1 - c))
    return pl.pallas_call(
        _mlstm_kernel,
        grid=(b, nblk),
        in_specs=[tok(0, False), ktf, tok(1, False), tok(0, True), ktb, tok(1, True),
                  *_chunk_specs((nr, CHUNK), nblk), *_chunk_specs((CHUNK, CHUNK), nblk)],
        out_specs=[tok(0, False), tok(0, True)],
        out_shape=[jax.ShapeDtypeStruct((b, s, w), BF16)] * 2,
        scratch_shapes=[pltpu.VMEM((2 * MLSTM_HEADS, MLSTM_DIM, 2 * MLSTM_DIM), F32)],
        compiler_params=_cparams("parallel", "arbitrary"),
        name="mlstm",
    )(qvo, kt, qvo, qvo, kt, qvo, rows, rows, cols, cols)


def _halo_specs(tm, halo, width, s, col=0, groups=None):
    r = tm // halo
    nb = s // halo
    rows = [(tm, lambda m: m), (halo, lambda m: jnp.maximum(m * r - 1, 0)),
            (halo, lambda m: jnp.minimum((m + 1) * r, nb - 1))]
    if groups is None:
        return [pl.BlockSpec((1, n, width), lambda i, m, f=f: (i, f(m), col)) for n, f in rows]
    return [pl.BlockSpec((1, groups, n, width), lambda i, m, f=f: (i, 0, f(m), 0)) for n, f in rows]


def _ext(refs, g=None):
    idx = (0,) if g is None else (0, g)
    mid, top, bot = (r[idx].astype(F32) for r in refs)
    return jnp.concatenate([top[HALO_T - EXT:], mid, bot[:EXT]], axis=0)


def _ffn_body(xa, g_ref, wu_ref, cw_ref, cb_ref, wd_ref, fg_ref, y_ref, a_scr, final_norm):
    m = pl.program_id(1)
    tm = y_ref.shape[1]
    fh = wd_ref.shape[0]
    n_ext = xa.shape[0]
    row = lax.broadcasted_iota(jnp.int32, (n_ext, 1), 0)
    first = jnp.where(m > 0, 0, EXT)
    last = jnp.where(m < pl.num_programs(1) - 1, n_ext, EXT + tm)
    xn = _rms(xa, g_ref[...])
    xe = jnp.where(row >= first, jnp.where(row < last, xn, 0.0), 0.0).astype(BF16)

    def conv3(h, c0):
        w = cw_ref[:, c0:c0 + FFN_HC]
        out = (w[0:1] * pltpu.roll(h, 1, axis=0) + w[1:2] * h + w[2:3] * pltpu.roll(h, n_ext - 1, axis=0))
        return out[EXT:EXT + tm] + cb_ref[:, c0:c0 + FFN_HC]

    for j in range(fh // FFN_HC):
        c0 = j * FFN_HC
        gate = conv3(_dot(xe, wu_ref[:, c0:c0 + FFN_HC]), c0)
        val = conv3(_dot(xe, wu_ref[:, fh + c0:fh + c0 + FFN_HC]), fh + c0)
        a_scr[:, c0:c0 + FFN_HC] = (_silu(gate) * val).astype(BF16)
    y = xa[EXT:EXT + tm] + _dot(a_scr[...], wd_ref[...])
    if final_norm:
        y = _rms(y, fg_ref[...])
    y_ref[0] = y


def _a_tail_kernel(*refs, final_norm):
    x, fn, hf, hb, o = (refs[3 * i:3 * i + 3] for i in range(5))
    wo_ref, g_ref, wu_ref, cw_ref, cb_ref, wd_ref, fg_ref, y_ref, a_scr = refs[15:]
    ng = fn[0].shape[1]
    mix = [_ext(fn, g) for g in range(ng)] + [jax.nn.sigmoid(_ext(o)) * (_ext(hf) + _ext(hb))]
    xa = _ext(x) + _dot(jnp.concatenate(mix, axis=1).astype(BF16), wo_ref[...])
    _ffn_body(xa, g_ref, wu_ref, cw_ref, cb_ref, wd_ref, fg_ref, y_ref, a_scr, final_norm)


def _c_tail_kernel(*refs, final_norm):
    x, yf, yb, xs, z, yc = (refs[3 * i:3 * i + 3] for i in range(6))
    dsk_ref, sng_ref, wo_ref, g_ref, wu_ref, cw_ref, cb_ref, wd_ref, fg_ref, y_ref, a_scr = refs[18:]
    y = (_ext(yf) + _ext(yb) + dsk_ref[...] * _ext(xs)) * _silu(_ext(z))
    mix = jnp.concatenate([_rms(y, sng_ref[...]), _ext(yc)], axis=1).astype(BF16)
    xa = _ext(x) + _dot(mix, wo_ref[...])
    _ffn_body(xa, g_ref, wu_ref, cw_ref, cb_ref, wd_ref, fg_ref, y_ref, a_scr, final_norm)


def _tail_call(kernel, name, x, tok_specs, tok_args, consts, wd, final_norm):
    b, s, d = x.shape
    tm = min(TM_FFN, s)
    return pl.pallas_call(
        functools.partial(kernel, final_norm=final_norm),
        grid=(b, s // tm),
        in_specs=tok_specs + [_const_spec(a.shape) for a in consts],
        out_specs=pl.BlockSpec((1, tm, d), lambda i, m: (i, m, 0)),
        out_shape=jax.ShapeDtypeStruct((b, s, d), F32),
        scratch_shapes=[pltpu.VMEM((tm, wd.shape[0]), BF16)],
        compiler_params=_cparams("parallel", "parallel"),
        name=name,
    )(*tok_args, *consts)


def _a_tail(x, fn, hf, hb, qvo, wo, ffn, final_norm):
    b, s, d = x.shape
    tm = min(TM_FFN, s)
    mw = hf.shape[2]
    hs = functools.partial(_halo_specs, tm, HALO_T)
    specs = (hs(d, s) + hs(fn.shape[3], s, groups=fn.shape[1]) + hs(mw, s) + hs(mw, s) + hs(mw, s, col=2))
    args = [a for a in (x, fn, hf, hb, qvo) for _ in range(3)]
    return _tail_call(_a_tail_kernel, "a_tail", x, specs, args, [wo] + ffn, ffn[4], final_norm)


def _c_tail(x, yf, yb, xbc, z, yc, dsk, sng, wo, ffn, final_norm):
    b, s, d = x.shape
    tm = min(TM_FFN, s)
    sw = yf.shape[2]
    hs = functools.partial(_halo_specs, tm, HALO_T)
    specs = hs(d, s) + hs(sw, s) + hs(sw, s) + hs(sw, s) + hs(sw, s) + hs(yc.shape[2], s)
    args = [a for a in (x, yf, yb, xbc, z, yc) for _ in range(3)]
    return _tail_call(_c_tail_kernel, "c_tail", x, specs, args, [dsk, sng, wo] + ffn, ffn[4], final_norm)


def _dwconv_taps(xe, w, tm, halo):
    k = w.shape[0]
    pad = (k - 1) // 2
    n = xe.shape[0]
    acc = None
    for j in range(k):
        off = halo - pad + j
        tap = w[j:j + 1] * pltpu.roll(xe, (n - off) % n, axis=0)[0:tm]
        acc = tap if acc is None else acc + tap
    return acc


def _c_in_kernel(x_ref, top_ref, bot_ref, g_ref, w_ref, wdt_ref, scw_ref, scb_ref, ccw_ref, ccb_ref, lg_ref, lb_ref,
                 z_ref, xbc_ref, yc_ref, dtt_ref):
    m = pl.program_id(1)
    tm = x_ref.shape[1]
    g = g_ref[...]
    zw = z_ref.shape[2]
    xw = xbc_ref.shape[2]
    cw = yc_ref.shape[2]
    ct = 512
    xm = _rms(x_ref[0], g)
    top = jnp.where(m > 0, _rms(top_ref[0], g), 0.0)
    bot = jnp.where(m < pl.num_programs(1) - 1, _rms(bot_ref[0], g), 0.0)
    xb = xm.astype(BF16)
    xe = jnp.concatenate([top, xm, bot], axis=0).astype(BF16)
    for j in range(zw // ct):
        sl = slice(j * ct, (j + 1) * ct)
        z_ref[0, :, sl] = _dot(xb, w_ref[:, sl]).astype(BF16)
    dtt = lax.dot_general(wdt_ref[...], xb, _NT, preferred_element_type=F32)
    for c in range(dtt_ref.shape[1]):
        dtt_ref[0, c] = dtt[:, c * CHUNK:(c + 1) * CHUNK]
    for j in range(xw // ct):
        sl = slice(j * ct, (j + 1) * ct)
        h = _dot(xe, w_ref[:, zw + j * ct:zw + (j + 1) * ct])
        xbc_ref[0, :, sl] = _silu(_dwconv_taps(h, scw_ref[:, sl], tm, HALO_CONF) + scb_ref[:, sl]).astype(BF16)
    val = _dot(xe, w_ref[:, zw + xw:zw + xw + cw])
    gate = _dot(xe, w_ref[:, zw + xw + cw:zw + xw + 2 * cw])
    u = _dwconv_taps(val * jax.nn.sigmoid(gate), ccw_ref[...], tm, HALO_CONF) + ccb_ref[...]
    mu = jnp.mean(u, axis=-1, keepdims=True)
    uc = u - mu
    var = jnp.mean(uc * uc, axis=-1, keepdims=True)
    yc_ref[0] = _silu(uc * lax.rsqrt(var + EPS) * lg_ref[...] + lb_ref[...]).astype(BF16)


def _c_in(x, g, w, wdt, scw, scb, ccw, ccb, lg, lb):
    b, s, d = x.shape
    tm = min(TM_PROJ, s)
    nd = wdt.shape[0]
    xw = scw.shape[1]
    cw = ccw.shape[1]
    zw = w.shape[1] - xw - 2 * cw
    mid, top, bot = _halo_specs(tm, HALO_CONF, d, s)
    tok = lambda width: pl.BlockSpec((1, tm, width), lambda i, m: (i, m, 0))
    consts = [g, w, wdt, scw, scb, ccw, ccb, lg, lb]
    return pl.pallas_call(
        _c_in_kernel,
        grid=(b, s // tm),
        in_specs=[mid, top, bot] + [_const_spec(a.shape) for a in consts],
        out_specs=[tok(zw), tok(xw), tok(cw),
                   pl.BlockSpec((1, tm // CHUNK, nd, CHUNK), lambda i, m: (i, m, 0, 0))],
        out_shape=[jax.ShapeDtypeStruct((b, s, zw), BF16), jax.ShapeDtypeStruct((b, s, xw), BF16),
                   jax.ShapeDtypeStruct((b, s, cw), BF16), jax.ShapeDtypeStruct((b, s // CHUNK, nd, CHUNK), F32)],
        compiler_params=_cparams("parallel", "parallel"),
        name="c_in",
    )(x, x, x, *consts)


def _ssd_gates_kernel(dt_ref, bias_ref, aneg_ref, rows_ref, cols_ref, stk_scr):
    nh = SSD_GROUPS * SSD_HPG
    nr = 2 * nh
    nc = dt_ref.shape[1]
    dt = _softplus(dt_ref[0] + bias_ref[...][None])
    da = dt * aneg_ref[...][None]
    isf = lax.broadcasted_iota(jnp.int32, dt.shape, 1) < nh
    pre, suf = _scan_lanes(da, jnp.add, 0.0)
    acs = jnp.where(isf, pre, suf)
    tot = pre + suf - da
    rows_ref[0, :, 0:nr] = dt
    rows_ref[0, :, nr:2 * nr] = acs - jnp.log(dt)
    rows_ref[0, :, 2 * nr:3 * nr] = jnp.exp(tot - acs) * dt
    rows_ref[0, :, 3 * nr:4 * nr] = jnp.exp(tot)
    stk_scr[:, 0:nr] = acs
    stk_scr[:, nr:] = jnp.zeros((nc, CHUNK - nr, CHUNK), F32)
    _cols_out(cols_ref, stk_scr)


def _ssd_gates(dtt, dtbias_col, aneg_col):
    b, nc, nr, _ = dtt.shape
    assert 4 * nr == CHUNK
    full = pl.BlockSpec((1, nc, CHUNK, CHUNK), lambda i: (i, 0, 0, 0))
    return pl.pallas_call(
        _ssd_gates_kernel,
        grid=(b,),
        in_specs=[pl.BlockSpec((1, nc, nr, CHUNK), lambda i: (i, 0, 0, 0)), _const_spec(dtbias_col.shape),
                  _const_spec(aneg_col.shape)],
        out_specs=[full, full],
        out_shape=[jax.ShapeDtypeStruct((b, nc, CHUNK, CHUNK), F32)] * 2,
        scratch_shapes=[pltpu.VMEM((nc, CHUNK, CHUNK), F32)],
        compiler_params=_cparams("parallel"),
        name="ssd_gates",
    )(dtt, dtbias_col, aneg_col)


def _ssd_kernel(xf_ref, bf_ref, cf_ref, xb_ref, bb_ref, cb_ref, rf_ref, rb_ref, kf_ref, kb_ref,
                yf_ref, yb_ref, hst_ref):
    ng, nj, hp, ns = SSD_GROUPS, SSD_HPG, SSD_P, SSD_N
    nh = ng * nj
    nr = 2 * nh
    gw = nj * hp
    L = CHUNK

    @pl.when(pl.program_id(1) == 0)
    def _():
        hst_ref[...] = jnp.zeros_like(hst_ref)

    t_i = lax.broadcasted_iota(jnp.int32, (L, L), 0)
    s_i = lax.broadcasted_iota(jnp.int32, (L, L), 1)
    hid = lax.broadcasted_iota(jnp.int32, (1, gw), 1) // hp
    hmask = [(hid == j).astype(F32).astype(BF16) for j in range(nj)]
    low = lax.broadcasted_iota(jnp.int32, (1, 2 * hp), 1) < hp

    def pair_lanes(vals):
        return jnp.concatenate([jnp.where(low, vals[2 * i], vals[2 * i + 1]) for i in range(nj // 2)], axis=1)

    for d in range(2):
        x_ref, b_ref, c_ref, r_ref, k_ref, y_ref = ((xf_ref, bf_ref, cf_ref, rf_ref, kf_ref, yf_ref) if d == 0 else
                                                    (xb_ref, bb_ref, cb_ref, rb_ref, kb_ref, yb_ref))
        tri = (s_i <= t_i) if d == 0 else (s_i >= t_i)
        for step in range(CPS):
            ci = step if d == 0 else CPS - 1 - step
            tsl = slice(ci * L, (ci + 1) * L)
            for g in range(ng):
                u = d * ng + g
                rows = [d * nh + g * nj + j for j in range(nj)]
                x = x_ref[0, tsl, g * gw:(g + 1) * gw]
                bm = b_ref[0, tsl, g * ns:(g + 1) * ns]
                cm = c_ref[0, tsl, g * ns:(g + 1) * ns]
                cbm = lax.dot_general(cm, bm, _NT, preferred_element_type=F32)
                bmt = bm.T.astype(F32)
                yd = None
                st = None
                esc = []
                for i in range(nj // 2):
                    ws, bws, xs = [], [], []
                    for j in (2 * i, 2 * i + 1):
                        r = rows[j]
                        acs_c = jnp.broadcast_to(k_ref[0, ci, :, r:r + 1], (L, L))
                        dec = jnp.exp(jnp.where(tri, acs_c - r_ref[0, ci, nr + r:nr + r + 1, :], -jnp.inf))
                        ws.append((cbm * dec).astype(BF16))
                        bws.append((bmt * r_ref[0, ci, 2 * nr + r:2 * nr + r + 1, :]).astype(BF16))
                        xs.append(x * hmask[j])
                        esc.append(jnp.exp(acs_c))
                    xk = jnp.concatenate(xs, axis=0)
                    part = _dot(jnp.concatenate(ws, axis=1), xk)
                    spart = _dot(jnp.concatenate(bws, axis=1), xk)
                    yd = part if yd is None else yd + part
                    st = spart if st is None else st + spart
                hprev = hst_ref[u]
                yoff = _dot(cm, hprev.astype(BF16))
                y_ref[0, tsl, g * gw:(g + 1) * gw] = (yd + yoff * pair_lanes(esc)).astype(y_ref.dtype)
                dsc = pair_lanes([r_ref[0, ci, 3 * nr + r:3 * nr + r + 1, :] for r in rows])
                hst_ref[u] = dsc * hprev + st


def _ssd(xbc, rows, cols):
    b, s, _ = xbc.shape
    tl = CPS * CHUNK
    nblk = s // tl
    xw = SSD_GROUPS * SSD_HPG * SSD_P
    bw = SSD_GROUPS * SSD_N
    assert xw == 2 * bw and 2 * SSD_P == LANES

    def spec(width, j, rev):
        if rev:
            return pl.BlockSpec((1, tl, width), lambda i, c: (i, nblk - 1 - c, j))
        return pl.BlockSpec((1, tl, width), lambda i, c: (i, c, j))

    return pl.pallas_call(
        _ssd_kernel,
        grid=(b, nblk),
        in_specs=[spec(xw, 0, False), spec(bw, 2, False), spec(bw, 3, False),
                  spec(xw, 0, True), spec(bw, 2, True), spec(bw, 3, True),
                  *_chunk_specs((CHUNK, CHUNK), nblk), *_chunk_specs((CHUNK, CHUNK), nblk)],
        out_specs=[spec(xw, 0, False), spec(xw, 0, True)],
        out_shape=[jax.ShapeDtypeStruct((b, s, xw), BF16)] * 2,
        scratch_shapes=[pltpu.VMEM((2 * SSD_GROUPS, SSD_N, SSD_HPG * SSD_P), F32)],
        compiler_params=_cparams("parallel", "arbitrary"),
        name="ssd",
    )(xbc, xbc, xbc, xbc, xbc, xbc, rows, rows, cols, cols)


def _dft_consts(s):
    s1 = s // LANES
    c = np.arange(FNET_DIM)
    ang = 2.0 * np.pi * np.outer(c, c) / FNET_DIM
    dft = np.concatenate([np.cos(ang), -np.sin(ang)], axis=1)
    k1 = np.arange(s1)
    a1 = 2.0 * np.pi * np.outer(k1, k1) / s1
    c1, sn1 = np.cos(a1), np.sin(a1)
    w1 = np.block([[c1, sn1], [-sn1, c1]])
    s2 = np.arange(LANES)
    at = 2.0 * np.pi * (s2[:, None] * k1[None, :]) / s
    tc = np.broadcast_to(np.cos(at)[:, :, None], (LANES, s1, LANES))
    ts = np.broadcast_to(np.sin(at)[:, :, None], (LANES, s1, LANES))
    a2 = 2.0 * np.pi * np.outer(s2, s2) / LANES
    w2 = np.concatenate([np.cos(a2), np.sin(a2)], axis=1)
    scale = 1.0 / math.sqrt(s * FNET_DIM)
    return (jnp.asarray(dft, BF16), jnp.asarray(w1, BF16), jnp.asarray(tc, F32), jnp.asarray(ts, F32),
            jnp.asarray(w2, BF16), scale)


def _encode(x, p, consts):
    dft, w1, tc, ts, w2, fscale = consts
    v, qvo, kt, gt = _a_in(x, p["a_norm"], p["a_wf"], dft, p["a_wq"], p["a_wkt"], p["a_wgt"])
    fn = _fft2(_fft1(v, w1, tc, ts), w2, fscale)
    hf, hb = _mlstm(qvo, kt, *_mlstm_gates(gt, p["a_gbias"]))
    ffn = lambda i: [p["f_norm"][i], p["f_wu"][i], p["f_cw"][i], p["f_cb"][i], p["f_wd"][i], p["final"]]
    x = _a_tail(x, fn, hf, hb, qvo, p["a_wout"], ffn(0), False)
    z, xbc, yc, dtt = _c_in(x, p["c_norm"], p["c_w"], p["c_wdt"], p["c_scw"], p["c_scb"], p["c_ccw"], p["c_ccb"],
                            p["c_lng"], p["c_lnb"])
    yf, yb = _ssd(xbc, *_ssd_gates(dtt, p["c_dtbias"], p["c_aneg"]))
    return _c_tail(x, yf, yb, xbc, z, yc, p["c_dsk"], p["c_sng"], p["c_wout"], ffn(1), True)


def _prep(a_norm, a_w_in, a_gate_bias, a_w_out, c_norm, c_w_in, c_ssd_conv_w, c_ssd_conv_b, c_dt_bias, c_a_log,
          c_d_skip, c_ssd_norm, c_conf_conv_w, c_conf_conv_b, c_conf_ln_g, c_conf_ln_b, c_w_out,
          ffn_norm, ffn_w_up, ffn_conv_w, ffn_conv_b, ffn_w_down, final_norm):
    fw = FNET_GROUPS * FNET_DIM
    mw = MLSTM_HEADS * MLSTM_DIM
    zw = SSD_GROUPS * SSD_HPG * SSD_P
    xw = zw + 2 * SSD_GROUPS * SSD_N
    ndt = 2 * SSD_GROUPS * SSD_HPG
    wa = a_w_in[0]
    wc = c_w_in[0]
    row = lambda v: v.reshape(1, -1).astype(F32)
    return {
        "a_norm": row(a_norm[0]),
        "a_wf": wa[:, :fw].astype(BF16),
        "a_wq": jnp.concatenate([wa[:, fw:fw + mw], wa[:, fw + 2 * mw:fw + 4 * mw]], axis=1).astype(BF16),
        "a_wkt": wa[:, fw + mw:fw + 2 * mw].T.astype(BF16),
        "a_wgt": wa[:, fw + 4 * mw:].T.astype(BF16),
        "a_gbias": a_gate_bias[0].reshape(-1, 1).astype(F32),
        "a_wout": a_w_out[0].astype(BF16),
        "c_norm": row(c_norm[0]),
        "c_w": jnp.concatenate([wc[:, :zw + xw], wc[:, zw + xw + ndt:]], axis=1).astype(BF16),
        "c_wdt": wc[:, zw + xw:zw + xw + ndt].T.astype(BF16),
        "c_scw": c_ssd_conv_w[0].astype(F32),
        "c_scb": row(c_ssd_conv_b[0]),
        "c_dtbias": c_dt_bias[0].reshape(-1, 1).astype(F32),
        "c_aneg": (-jnp.exp(c_a_log[0].astype(F32))).reshape(-1, 1),
        "c_dsk": jnp.repeat(c_d_skip[0].reshape(-1).astype(F32), SSD_P).reshape(1, -1),
        "c_sng": row(c_ssd_norm[0]),
        "c_ccw": c_conf_conv_w[0].astype(F32),
        "c_ccb": row(c_conf_conv_b[0]),
        "c_lng": row(c_conf_ln_g[0]),
        "c_lnb": row(c_conf_ln_b[0]),
        "c_wout": c_w_out[0].astype(BF16),
        "f_norm": [row(ffn_norm[i]) for i in range(2)],
        "f_wu": [ffn_w_up[i].astype(BF16) for i in range(2)],
        "f_cw": [ffn_conv_w[i].astype(F32) for i in range(2)],
        "f_cb": [row(ffn_conv_b[i]) for i in range(2)],
        "f_wd": [ffn_w_down[i].astype(BF16) for i in range(2)],
        "final": row(final_norm),
    }


def kernel(x_prompt, x_sample, a_norm, a_w_in, a_gate_bias, a_w_out, c_norm, c_w_in, c_ssd_conv_w, c_ssd_conv_b, c_dt_bias, c_a_log, c_d_skip, c_ssd_norm, c_conf_conv_w, c_conf_conv_b, c_conf_ln_g, c_conf_ln_b, c_w_out, ffn_norm, ffn_w_up, ffn_conv_w, ffn_conv_b, ffn_w_down, final_norm):
    assert a_norm.shape[0] == 1 and c_norm.shape[0] == 1 and ffn_norm.shape[0] == 2
    p = _prep(a_norm, a_w_in, a_gate_bias, a_w_out, c_norm, c_w_in, c_ssd_conv_w, c_ssd_conv_b, c_dt_bias, c_a_log,
              c_d_skip, c_ssd_norm, c_conf_conv_w, c_conf_conv_b, c_conf_ln_g, c_conf_ln_b, c_w_out,
              ffn_norm, ffn_w_up, ffn_conv_w, ffn_conv_b, ffn_w_down, final_norm)
    outs = []
    consts = {}
    for x in (x_prompt, x_sample):
        s = x.shape[1]
        assert s % (LANES * SUBLANES) == 0
        if s not in consts:
            consts[s] = _dft_consts(s)
        outs.append(_encode(x.astype(F32), p, consts[s]))
    return tuple(outs)
```

```python
import functools
import math

import numpy as np
import jax
import jax.numpy as jnp
from jax import lax
from jax.experimental import pallas as pl
from jax.experimental.pallas import tpu as pltpu

F32 = jnp.float32
BF16 = jnp.bfloat16
EPS = 1e-6

LANES = 128
SUBLANES = 8
CHUNK = 128
VMEM_LIMIT = 56 * 1024 * 1024

FNET_GROUPS = 4
FNET_DIM = 128
MLSTM_HEADS = 4
MLSTM_DIM = 128
SSD_GROUPS = 4
SSD_HPG = 4
SSD_P = 64
SSD_N = 128
CONF_W = 512

TM_PROJ = 512
TM_FFN = 512
HALO_CONF = 16
HALO_T = 16
EXT = 8
CPS = 4
FFN_HC = 512

_NT = (((1,), (1,)), ((), ()))
_TN = (((0,), (0,)), ((), ()))


def _cparams(*sem):
    return pltpu.CompilerParams(dimension_semantics=sem, vmem_limit_bytes=VMEM_LIMIT)


def _const_spec(shape):
    nd = len(shape)
    return pl.BlockSpec(shape, lambda *_: (0,) * nd, pipeline_mode=pl.Buffered(1))


def _dot(a, b):
    return jnp.dot(a, b, preferred_element_type=F32)


def _rms(x, g):
    return x * lax.rsqrt(jnp.mean(x * x, axis=-1, keepdims=True) + EPS) * g


def _softplus(x):
    return jnp.maximum(x, 0.0) + jnp.log1p(jnp.exp(-jnp.abs(x)))


def _log_sigmoid(x):
    return -_softplus(-x)


def _silu(x):
    return x * jax.nn.sigmoid(x)


def _scan_lanes(x, combine, fill):
    ax = x.ndim - 1
    lane = lax.broadcasted_iota(jnp.int32, x.shape, ax)
    pre, suf = x, x
    k = 1
    while k < CHUNK:
        pre = combine(pre, jnp.where(lane >= k, pltpu.roll(pre, k, axis=ax), fill))
        suf = combine(suf, jnp.where(lane < CHUNK - k, pltpu.roll(suf, CHUNK - k, axis=ax), fill))
        k *= 2
    return pre, suf


def _cols_out(cols_ref, stk_scr):
    def body(c, carry):
        cols_ref[0, c] = stk_scr[c].T
        return carry
    lax.fori_loop(0, stk_scr.shape[0], body, 0, unroll=4)


def _pack2(hi, lo):
    hb = lax.bitcast_convert_type(hi.astype(BF16).astype(F32), jnp.uint32)
    lb = lax.bitcast_convert_type(lo.astype(BF16).astype(F32), jnp.uint32)
    return hb | (lb >> 16)


def _unpack2(w):
    hi = lax.bitcast_convert_type(w & jnp.uint32(0xFFFF0000), F32)
    lo = lax.bitcast_convert_type(w << 16, F32)
    return hi.astype(BF16), lo.astype(BF16)


def _chunk_specs(shape, nblk):
    fwd = pl.BlockSpec((1, CPS) + shape, lambda i, c: (i, c, 0, 0))
    bwd = pl.BlockSpec((1, CPS) + shape, lambda i, c: (i, nblk - 1 - c, 0, 0))
    return fwd, bwd


def _a_in_kernel(x_ref, g_ref, wf_ref, dft_ref, wq_ref, wkt_ref, wgt_ref, v_ref, qvo_ref, kt_ref, gt_ref):
    xb = _rms(x_ref[0], g_ref[...]).astype(BF16)
    hf = _dot(xb, wf_ref[...]).astype(BF16)
    for g in range(FNET_GROUPS):
        pq = _dot(hf[:, g * FNET_DIM:(g + 1) * FNET_DIM], dft_ref[...])
        v_ref[0, g] = _pack2(pq[:, :FNET_DIM], pq[:, FNET_DIM:])
    mw = MLSTM_HEADS * MLSTM_DIM
    qvo_ref[0, :, 0:mw] = (_dot(xb, wq_ref[:, 0:mw]) * (MLSTM_DIM ** -0.5)).astype(BF16)
    for j in range(1, qvo_ref.shape[2] // mw):
        sl = slice(j * mw, (j + 1) * mw)
        qvo_ref[0, :, sl] = _dot(xb, wq_ref[:, sl]).astype(BF16)
    kt_ref[0] = lax.dot_general(wkt_ref[...], xb, _NT, preferred_element_type=F32).astype(BF16)
    gt = lax.dot_general(wgt_ref[...], xb, _NT, preferred_element_type=F32)
    for c in range(gt_ref.shape[1]):
        gt_ref[0, c] = gt[:, c * CHUNK:(c + 1) * CHUNK]


def _a_in(x, g, wf, dft, wq, wkt, wgt):
    b, s, d = x.shape
    tm = min(TM_PROJ, s)
    fw = wf.shape[1]
    nq = wq.shape[1]
    kw = wkt.shape[0]
    ng = wgt.shape[0]
    return pl.pallas_call(
        _a_in_kernel,
        grid=(b, s // tm),
        in_specs=[
            pl.BlockSpec((1, tm, d), lambda i, m: (i, m, 0)),
            _const_spec((1, d)), _const_spec(wf.shape), _const_spec(dft.shape),
            _const_spec(wq.shape), _const_spec(wkt.shape), _const_spec(wgt.shape),
        ],
        out_specs=[
            pl.BlockSpec((1, FNET_GROUPS, tm, FNET_DIM), lambda i, m: (i, 0, m, 0)),
            pl.BlockSpec((1, tm, nq), lambda i, m: (i, m, 0)),
            pl.BlockSpec((1, kw, tm), lambda i, m: (i, 0, m)),
            pl.BlockSpec((1, tm // CHUNK, ng, CHUNK), lambda i, m: (i, m, 0, 0)),
        ],
        out_shape=[
            jax.ShapeDtypeStruct((b, FNET_GROUPS, s, FNET_DIM), jnp.uint32),
            jax.ShapeDtypeStruct((b, s, nq), BF16),
            jax.ShapeDtypeStruct((b, kw, s), BF16),
            jax.ShapeDtypeStruct((b, s // CHUNK, ng, CHUNK), F32),
        ],
        compiler_params=_cparams("parallel", "parallel"),
        name="a_in",
    )(x, g, wf, dft, wq, wkt, wgt)


def _fft1_kernel(v_ref, w1_ref, tc_ref, ts_ref, y_ref, vflat):
    ng, s1, ts2 = v_ref.shape[1:4]
    vflat[...] = v_ref[0].reshape(ng, s1 * ts2, LANES)
    for i in range(ts2):
        vr, vi = zip(*[_unpack2(vflat[g, pl.ds(i, s1, stride=ts2), :]) for g in range(ng)])
        v = jnp.concatenate([jnp.concatenate(vr, axis=1), jnp.concatenate(vi, axis=1)], axis=0)
        y = _dot(w1_ref[...], v)
        tc = tc_ref[i]
        ts = ts_ref[i]
        for g in range(ng):
            yr = y[:s1, g * LANES:(g + 1) * LANES]
            yi = y[s1:, g * LANES:(g + 1) * LANES]
            y_ref[0, g, i] = _pack2(yr * tc + yi * ts, yi * tc - yr * ts)


def _fft1(v, w1, tc, ts):
    b, ng, s, _ = v.shape
    s1 = s // LANES
    ts2 = SUBLANES
    v5 = v.reshape(b, ng, s1, LANES, LANES)
    spec_tw = pl.BlockSpec((ts2, s1, LANES), lambda j, i: (j, 0, 0))
    return pl.pallas_call(
        _fft1_kernel,
        grid=(LANES // ts2, b),
        in_specs=[pl.BlockSpec((1, ng, s1, ts2, LANES), lambda j, i: (i, 0, 0, j, 0)),
                  _const_spec(w1.shape), spec_tw, spec_tw],
        out_specs=pl.BlockSpec((1, ng, ts2, s1, LANES), lambda j, i: (i, 0, j, 0, 0)),
        out_shape=jax.ShapeDtypeStruct((b, ng, LANES, s1, LANES), jnp.uint32),
        scratch_shapes=[pltpu.VMEM((ng, s1 * ts2, LANES), jnp.uint32)],
        compiler_params=_cparams("parallel", "parallel"),
        name="fft1",
    )(v5, w1, tc, ts)


def _fft2_kernel(y_ref, w2_ref, o_ref, yflat, oflat, *, scale):
    ng, ns2, tk1 = y_ref.shape[1:4]
    yflat[...] = y_ref[0].reshape(ng, ns2 * tk1, LANES)
    for i in range(tk1):
        yr, yi = zip(*[_unpack2(yflat[g, pl.ds(i, ns2, stride=tk1), :]) for g in range(ng)])
        y = jnp.concatenate([jnp.concatenate(yr, axis=1), jnp.concatenate(yi, axis=1)], axis=0)
        out = _dot(w2_ref[...], y) * scale
        for g in range(ng):
            oflat[g, pl.ds(i, ns2, stride=tk1), :] = out[:, g * LANES:(g + 1) * LANES]
    o_ref[0] = oflat[...].reshape(ng, ns2, tk1, LANES)


def _fft2(y, w2, scale):
    b, ng, _, s1, _ = y.shape
    tk1 = SUBLANES
    spec = pl.BlockSpec((1, ng, LANES, tk1, LANES), lambda i, j: (i, 0, 0, j, 0))
    out = pl.pallas_call(
        functools.partial(_fft2_kernel, scale=scale),
        grid=(b, s1 // tk1),
        in_specs=[spec, _const_spec(w2.shape)],
        out_specs=spec,
        out_shape=jax.ShapeDtypeStruct((b, ng, LANES, s1, LANES), F32),
        scratch_shapes=[pltpu.VMEM((ng, LANES * tk1, LANES), jnp.uint32),
                        pltpu.VMEM((ng, LANES * tk1, LANES), F32)],
        compiler_params=_cparams("parallel", "parallel"),
        name="fft2",
    )(y, w2)
    return out.reshape(b, ng, LANES * s1, LANES)


def _mlstm_gates_kernel(gt_ref, bias_ref, rows_ref, cols_ref, tot_scr, amax_scr, mpf_scr, mpb_scr, stk_scr):
    nh = MLSTM_HEADS
    nu = 2 * nh
    nc = gt_ref.shape[1]
    g = gt_ref[0]
    bias = bias_ref[...][None]
    li = g[:, 0:nu] + bias[:, 0:nu]
    lf = _log_sigmoid(g[:, nu:] + bias[:, nu:])
    isf = lax.broadcasted_iota(jnp.int32, li.shape, 1) < nh
    pre, suf = _scan_lanes(lf, jnp.add, 0.0)
    bcs = jnp.where(isf, pre, suf)
    tot = pre + suf - lf
    e = li - bcs
    a = tot - bcs + li
    pa, sa = _scan_lanes(a, jnp.maximum, -jnp.inf)
    amax = jnp.maximum(pa, sa)
    pe, se = _scan_lanes(e, jnp.maximum, -jnp.inf)
    cmax = jnp.where(isf, pe, se)
    tot_scr[...] = tot
    amax_scr[...] = amax

    def fwd(c, m):
        mpf_scr[c] = m
        return jnp.maximum(tot_scr[c] + m, amax_scr[c])

    def bwd(i, m):
        c = nc - 1 - i
        mpb_scr[c] = m
        return jnp.maximum(tot_scr[c] + m, amax_scr[c])

    zero = jnp.zeros((nu, CHUNK), F32)
    lax.fori_loop(0, nc, fwd, zero)
    lax.fori_loop(0, nc, bwd, zero)
    m_prev = jnp.where(isf, mpf_scr[...], mpb_scr[...])
    m_new = jnp.maximum(tot + m_prev, amax)
    mu = jnp.maximum(m_prev, cmax)
    rows_ref[0, :, 0:nu] = e
    rows_ref[0, :, nu:2 * nu] = jnp.exp(a - m_new)
    rows_ref[0, :, 2 * nu:3 * nu] = jnp.exp(tot + m_prev - m_new)
    stk_scr[:, 0:nu] = mu
    stk_scr[:, nu:2 * nu] = jnp.exp(m_prev - mu)
    stk_scr[:, 2 * nu:3 * nu] = jnp.exp(-(bcs + mu))
    stk_scr[:, 3 * nu:] = jnp.zeros((nc, CHUNK - 3 * nu, CHUNK), F32)
    _cols_out(cols_ref, stk_scr)


def _mlstm_gates(gt, bias_col):
    b, nc, ng, _ = gt.shape
    nu = ng // 2
    small = pltpu.VMEM((nc, nu, CHUNK), F32)
    return pl.pallas_call(
        _mlstm_gates_kernel,
        grid=(b,),
        in_specs=[pl.BlockSpec((1, nc, ng, CHUNK), lambda i: (i, 0, 0, 0)), _const_spec(bias_col.shape)],
        out_specs=[pl.BlockSpec((1, nc, 3 * nu, CHUNK), lambda i: (i, 0, 0, 0)),
                   pl.BlockSpec((1, nc, CHUNK, CHUNK), lambda i: (i, 0, 0, 0))],
        out_shape=[jax.ShapeDtypeStruct((b, nc, 3 * nu, CHUNK), F32),
                   jax.ShapeDtypeStruct((b, nc, CHUNK, CHUNK), F32)],
        scratch_shapes=[small, small, small, small, pltpu.VMEM((nc, CHUNK, CHUNK), F32)],
        compiler_params=_cparams("parallel"),
        name="mlstm_gates",
    )(gt, bias_col)


def _mlstm_kernel(qf_ref, ktf_ref, vf_ref, qb_ref, ktb_ref, vb_ref, rf_ref, rb_ref, cf_ref, cb_ref,
                  hf_ref, hb_ref, cst_ref):
    nh, dh = MLSTM_HEADS, MLSTM_DIM
    nu = 2 * nh
    L = CHUNK

    @pl.when(pl.program_id(1) == 0)
    def _():
        cst_ref[...] = jnp.zeros_like(cst_ref)

    t_i = lax.broadcasted_iota(jnp.int32, (L, L), 0)
    s_i = lax.broadcasted_iota(jnp.int32, (L, L), 1)
    top = lax.broadcasted_iota(jnp.int32, (2 * dh, L), 0) < dh
    ones = jnp.ones((L, dh), BF16)
    for d in range(2):
        q_ref, kt_ref, v_ref, r_ref, c_ref, o_ref = ((qf_ref, ktf_ref, vf_ref, rf_ref, cf_ref, hf_ref) if d == 0 else
                                                     (qb_ref, ktb_ref, vb_ref, rb_ref, cb_ref, hb_ref))
        tri = (s_i <= t_i) if d == 0 else (s_i >= t_i)
        for step in range(CPS):
            ci = step if d == 0 else CPS - 1 - step
            tsl = slice(ci * L, (ci + 1) * L)
            for hp in range(nh // 2):
                psl = slice(2 * hp * dh, (2 * hp + 2) * dh)
                kt2 = kt_ref[0, psl, tsl]
                zero = jnp.zeros_like(kt2)
                kbd = jnp.concatenate([jnp.where(top, kt2, zero), jnp.where(top, zero, kt2)], axis=1)
                sc2 = _dot(q_ref[0, tsl, psl], kbd)
                for hh in range(2):
                    h = 2 * hp + hh
                    u = d * nh + h
                    sl = slice(h * dh, (h + 1) * dh)
                    q = q_ref[0, tsl, sl]
                    v = v_ref[0, tsl, sl]
                    vaug = jnp.concatenate([v, ones], axis=1)
                    e_r = r_ref[0, ci, u:u + 1, :]
                    ea_r = r_ref[0, ci, nu + u:nu + u + 1, :]
                    decay = r_ref[0, ci, 2 * nu + u:2 * nu + u + 1, 0:1]
                    mu_c = c_ref[0, ci, :, u:u + 1]
                    isc_c = c_ref[0, ci, :, nu + u:nu + u + 1]
                    emt_c = c_ref[0, ci, :, 2 * nu + u:2 * nu + u + 1]
                    w = jnp.exp(jnp.where(tri, e_r - mu_c, -jnp.inf))
                    sw = (sc2[:, hh * L:(hh + 1) * L] * w).astype(BF16)
                    qi = (q.astype(F32) * isc_c).astype(BF16)
                    cs = cst_ref[u]
                    tot = _dot(jnp.concatenate([sw, qi], axis=1),
                               jnp.concatenate([vaug, cs.astype(BF16)], axis=0))
                    den = jnp.maximum(jnp.abs(tot[:, dh:]), emt_c)
                    o_ref[0, tsl, sl] = (tot[:, :dh] / den).astype(o_ref.dtype)
                    kw = (kt_ref[0, sl, tsl].astype(F32) * ea_r).astype(BF16)
                    cst_ref[u] = decay * cs + _dot(kw, vaug)


def _mlstm(qvo, kt, rows, cols):
    b, s, _ = qvo.shape
    tl = CPS * CHUNK
    nblk = s // tl
    w = MLSTM_HEADS * MLSTM_DIM
    nr = rows.shape[2]

    def tok(j, rev):
        if rev:
            return pl.BlockSpec((1, tl, w), lambda i, c: (i, nblk - 1 - c, j))
        return pl.BlockSpec((1, tl, w), lambda i, c: (i, c, j))

    ktf = pl.BlockSpec((1, w, tl), lambda i, c: (i, 0, c))
    ktb = pl.BlockSpec((1, w, tl), lambda i, c: (i, 0, nblk - 1 - c))
    return pl.pallas_call(
        _mlstm_kernel,
        grid=(b, nblk),
        in_specs=[tok(0, False), ktf, tok(1, False), tok(0, True), ktb, tok(1, True),
                  *_chunk_specs((nr, CHUNK), nblk), *_chunk_specs((CHUNK, CHUNK), nblk)],
        out_specs=[tok(0, False), tok(0, True)],
        out_shape=[jax.ShapeDtypeStruct((b, s, w), BF16)] * 2,
        scratch_shapes=[pltpu.VMEM((2 * MLSTM_HEADS, MLSTM_DIM, 2 * MLSTM_DIM), F32)],
        compiler_params=_cparams("parallel", "arbitrary"),
        name="mlstm",
    )(qvo, kt, qvo, qvo, kt, qvo, rows, rows, cols, cols)


def _halo_specs(tm, halo, width, s, col=0, groups=None):
    r = tm // halo
    nb = s // halo
    rows = [(tm, lambda m: m), (halo, lambda m: jnp.maximum(m * r - 1, 0)),
            (halo, lambda m: jnp.minimum((m + 1) * r, nb - 1))]
    if groups is None:
        return [pl.BlockSpec((1, n, width), lambda i, m, f=f: (i, f(m), col)) for n, f in rows]
    return [pl.BlockSpec((1, groups, n, width), lambda i, m, f=f: (i, 0, f(m), 0)) for n, f in rows]


def _ext(refs, g=None):
    idx = (0,) if g is None else (0, g)
    mid, top, bot = (r[idx].astype(F32) for r in refs)
    return jnp.concatenate([top[HALO_T - EXT:], mid, bot[:EXT]], axis=0)


def _ffn_body(xa, g_ref, wu_ref, cw_ref, cb_ref, wd_ref, fg_ref, y_ref, a_scr, final_norm):
    m = pl.program_id(1)
    tm = y_ref.shape[1]
    fh = wd_ref.shape[0]
    n_ext = xa.shape[0]
    row = lax.broadcasted_iota(jnp.int32, (n_ext, 1), 0)
    first = jnp.where(m > 0, 0, EXT)
    last = jnp.where(m < pl.num_programs(1) - 1, n_ext, EXT + tm)
    xn = _rms(xa, g_ref[...])
    xe = jnp.where(row >= first, jnp.where(row < last, xn, 0.0), 0.0).astype(BF16)

    def conv3(h, c0):
        w = cw_ref[:, c0:c0 + FFN_HC]
        out = (w[0:1] * pltpu.roll(h, 1, axis=0) + w[1:2] * h + w[2:3] * pltpu.roll(h, n_ext - 1, axis=0))
        return out[EXT:EXT + tm] + cb_ref[:, c0:c0 + FFN_HC]

    for j in range(fh // FFN_HC):
        c0 = j * FFN_HC
        gate = conv3(_dot(xe, wu_ref[:, c0:c0 + FFN_HC]), c0)
        val = conv3(_dot(xe, wu_ref[:, fh + c0:fh + c0 + FFN_HC]), fh + c0)
        a_scr[:, c0:c0 + FFN_HC] = (_silu(gate) * val).astype(BF16)
    y = xa[EXT:EXT + tm] + _dot(a_scr[...], wd_ref[...])
    if final_norm:
        y = _rms(y, fg_ref[...])
    y_ref[0] = y


def _a_tail_kernel(*refs, final_norm):
    x, fn, hf, hb, o = (refs[3 * i:3 * i + 3] for i in range(5))
    wo_ref, g_ref, wu_ref, cw_ref, cb_ref, wd_ref, fg_ref, y_ref, a_scr = refs[15:]
    ng = fn[0].shape[1]
    mix = [_ext(fn, g) for g in range(ng)] + [jax.nn.sigmoid(_ext(o)) * (_ext(hf) + _ext(hb))]
    xa = _ext(x) + _dot(jnp.concatenate(mix, axis=1).astype(BF16), wo_ref[...])
    _ffn_body(xa, g_ref, wu_ref, cw_ref, cb_ref, wd_ref, fg_ref, y_ref, a_scr, final_norm)


def _c_tail_kernel(*refs, final_norm):
    x, yf, yb, xs, z, yc = (refs[3 * i:3 * i + 3] for i in range(6))
    dsk_ref, sng_ref, wo_ref, g_ref, wu_ref, cw_ref, cb_ref, wd_ref, fg_ref, y_ref, a_scr = refs[18:]
    y = (_ext(yf) + _ext(yb) + dsk_ref[...] * _ext(xs)) * _silu(_ext(z))
    mix = jnp.concatenate([_rms(y, sng_ref[...]), _ext(yc)], axis=1).astype(BF16)
    xa = _ext(x) + _dot(mix, wo_ref[...])
    _ffn_body(xa, g_ref, wu_ref, cw_ref, cb_ref, wd_ref, fg_ref, y_ref, a_scr, final_norm)


def _tail_call(kernel, name, x, tok_specs, tok_args, consts, wd, final_norm):
    b, s, d = x.shape
    tm = min(TM_FFN, s)
    return pl.pallas_call(
        functools.partial(kernel, final_norm=final_norm),
        grid=(b, s // tm),
        in_specs=tok_specs + [_const_spec(a.shape) for a in consts],
        out_specs=pl.BlockSpec((1, tm, d), lambda i, m: (i, m, 0)),
        out_shape=jax.ShapeDtypeStruct((b, s, d), F32),
        scratch_shapes=[pltpu.VMEM((tm, wd.shape[0]), BF16)],
        compiler_params=_cparams("parallel", "parallel"),
        name=name,
    )(*tok_args, *consts)


def _a_tail(x, fn, hf, hb, qvo, wo, ffn, final_norm):
    b, s, d = x.shape
    tm = min(TM_FFN, s)
    mw = hf.shape[2]
    hs = functools.partial(_halo_specs, tm, HALO_T)
    specs = (hs(d, s) + hs(fn.shape[3], s, groups=fn.shape[1]) + hs(mw, s) + hs(mw, s) + hs(mw, s, col=2))
    args = [a for a in (x, fn, hf, hb, qvo) for _ in range(3)]
    return _tail_call(_a_tail_kernel, "a_tail", x, specs, args, [wo] + ffn, ffn[4], final_norm)


def _c_tail(x, yf, yb, xbc, z, yc, dsk, sng, wo, ffn, final_norm):
    b, s, d = x.shape
    tm = min(TM_FFN, s)
    sw = yf.shape[2]
    hs = functools.partial(_halo_specs, tm, HALO_T)
    specs = hs(d, s) + hs(sw, s) + hs(sw, s) + hs(sw, s) + hs(sw, s) + hs(yc.shape[2], s)
    args = [a for a in (x, yf, yb, xbc, z, yc) for _ in range(3)]
    return _tail_call(_c_tail_kernel, "c_tail", x, specs, args, [dsk, sng, wo] + ffn, ffn[4], final_norm)


def _dwconv_taps(xe, w, tm, halo):
    k = w.shape[0]
    pad = (k - 1) // 2
    n = xe.shape[0]
    acc = None
    for j in range(k):
        off = halo - pad + j
        tap = w[j:j + 1] * pltpu.roll(xe, (n - off) % n, axis=0)[0:tm]
        acc = tap if acc is None else acc + tap
    return acc


def _c_in_kernel(x_ref, top_ref, bot_ref, g_ref, w_ref, wdt_ref, scw_ref, scb_ref, ccw_ref, ccb_ref, lg_ref, lb_ref,
                 z_ref, xbc_ref, yc_ref, dtt_ref):
    m = pl.program_id(1)
    tm = x_ref.shape[1]
    g = g_ref[...]
    zw = z_ref.shape[2]
    xw = xbc_ref.shape[2]
    cw = yc_ref.shape[2]
    ct = 512
    xm = _rms(x_ref[0], g)
    top = jnp.where(m > 0, _rms(top_ref[0], g), 0.0)
    bot = jnp.where(m < pl.num_programs(1) - 1, _rms(bot_ref[0], g), 0.0)
    xb = xm.astype(BF16)
    xe = jnp.concatenate([top, xm, bot], axis=0).astype(BF16)
    for j in range(zw // ct):
        sl = slice(j * ct, (j + 1) * ct)
        z_ref[0, :, sl] = _dot(xb, w_ref[:, sl]).astype(BF16)
    dtt = lax.dot_general(wdt_ref[...], xb, _NT, preferred_element_type=F32)
    for c in range(dtt_ref.shape[1]):
        dtt_ref[0, c] = dtt[:, c * CHUNK:(c + 1) * CHUNK]
    for j in range(xw // ct):
        sl = slice(j * ct, (j + 1) * ct)
        h = _dot(xe, w_ref[:, zw + j * ct:zw + (j + 1) * ct])
        xbc_ref[0, :, sl] = _silu(_dwconv_taps(h, scw_ref[:, sl], tm, HALO_CONF) + scb_ref[:, sl]).astype(BF16)
    val = _dot(xe, w_ref[:, zw + xw:zw + xw + cw])
    gate = _dot(xe, w_ref[:, zw + xw + cw:zw + xw + 2 * cw])
    u = _dwconv_taps(val * jax.nn.sigmoid(gate), ccw_ref[...], tm, HALO_CONF) + ccb_ref[...]
    mu = jnp.mean(u, axis=-1, keepdims=True)
    uc = u - mu
    var = jnp.mean(uc * uc, axis=-1, keepdims=True)
    yc_ref[0] = _silu(uc * lax.rsqrt(var + EPS) * lg_ref[...] + lb_ref[...]).astype(BF16)


def _c_in(x, g, w, wdt, scw, scb, ccw, ccb, lg, lb):
    b, s, d = x.shape
    tm = min(TM_PROJ, s)
    nd = wdt.shape[0]
    xw = scw.shape[1]
    cw = ccw.shape[1]
    zw = w.shape[1] - xw - 2 * cw
    mid, top, bot = _halo_specs(tm, HALO_CONF, d, s)
    tok = lambda width: pl.BlockSpec((1, tm, width), lambda i, m: (i, m, 0))
    consts = [g, w, wdt, scw, scb, ccw, ccb, lg, lb]
    return pl.pallas_call(
        _c_in_kernel,
        grid=(b, s // tm),
        in_specs=[mid, top, bot] + [_const_spec(a.shape) for a in consts],
        out_specs=[tok(zw), tok(xw), tok(cw),
                   pl.BlockSpec((1, tm // CHUNK, nd, CHUNK), lambda i, m: (i, m, 0, 0))],
        out_shape=[jax.ShapeDtypeStruct((b, s, zw), BF16), jax.ShapeDtypeStruct((b, s, xw), BF16),
                   jax.ShapeDtypeStruct((b, s, cw), BF16), jax.ShapeDtypeStruct((b, s // CHUNK, nd, CHUNK), F32)],
        compiler_params=_cparams("parallel", "parallel"),
        name="c_in",
    )(x, x, x, *consts)


def _ssd_gates_kernel(dt_ref, bias_ref, aneg_ref, rows_ref, cols_ref, stk_scr):
    nh = SSD_GROUPS * SSD_HPG
    nr = 2 * nh
    nc = dt_ref.shape[1]
    dt = _softplus(dt_ref[0] + bias_ref[...][None])
    da = dt * aneg_ref[...][None]
    isf = lax.broadcasted_iota(jnp.int32, dt.shape, 1) < nh
    pre, suf = _scan_lanes(da, jnp.add, 0.0)
    acs = jnp.where(isf, pre, suf)
    tot = pre + suf - da
    rows_ref[0, :, 0:nr] = dt
    rows_ref[0, :, nr:2 * nr] = acs - jnp.log(dt)
    rows_ref[0, :, 2 * nr:3 * nr] = jnp.exp(tot - acs) * dt
    rows_ref[0, :, 3 * nr:4 * nr] = jnp.exp(tot)
    stk_scr[:, 0:nr] = acs
    stk_scr[:, nr:] = jnp.zeros((nc, CHUNK - nr, CHUNK), F32)
    _cols_out(cols_ref, stk_scr)


def _ssd_gates(dtt, dtbias_col, aneg_col):
    b, nc, nr, _ = dtt.shape
    assert 4 * nr == CHUNK
    full = pl.BlockSpec((1, nc, CHUNK, CHUNK), lambda i: (i, 0, 0, 0))
    return pl.pallas_call(
        _ssd_gates_kernel,
        grid=(b,),
        in_specs=[pl.BlockSpec((1, nc, nr, CHUNK), lambda i: (i, 0, 0, 0)), _const_spec(dtbias_col.shape),
                  _const_spec(aneg_col.shape)],
        out_specs=[full, full],
        out_shape=[jax.ShapeDtypeStruct((b, nc, CHUNK, CHUNK), F32)] * 2,
        scratch_shapes=[pltpu.VMEM((nc, CHUNK, CHUNK), F32)],
        compiler_params=_cparams("parallel"),
        name="ssd_gates",
    )(dtt, dtbias_col, aneg_col)


def _ssd_kernel(xf_ref, bf_ref, cf_ref, xb_ref, bb_ref, cb_ref, rf_ref, rb_ref, kf_ref, kb_ref,
                yf_ref, yb_ref, hst_ref):
    ng, nj, hp, ns = SSD_GROUPS, SSD_HPG, SSD_P, SSD_N
    nh = ng * nj
    nr = 2 * nh
    gw = nj * hp
    L = CHUNK

    @pl.when(pl.program_id(1) == 0)
    def _():
        hst_ref[...] = jnp.zeros_like(hst_ref)

    t_i = lax.broadcasted_iota(jnp.int32, (L, L), 0)
    s_i = lax.broadcasted_iota(jnp.int32, (L, L), 1)
    hid = lax.broadcasted_iota(jnp.int32, (1, gw), 1) // hp
    hmask = [(hid == j).astype(F32).astype(BF16) for j in range(nj)]
    low = lax.broadcasted_iota(jnp.int32, (1, 2 * hp), 1) < hp

    def pair_lanes(vals):
        return jnp.concatenate([jnp.where(low, vals[2 * i], vals[2 * i + 1]) for i in range(nj // 2)], axis=1)

    for d in range(2):
        x_ref, b_ref, c_ref, r_ref, k_ref, y_ref = ((xf_ref, bf_ref, cf_ref, rf_ref, kf_ref, yf_ref) if d == 0 else
                                                    (xb_ref, bb_ref, cb_ref, rb_ref, kb_ref, yb_ref))
        tri = (s_i <= t_i) if d == 0 else (s_i >= t_i)
        for step in range(CPS):
            ci = step if d == 0 else CPS - 1 - step
            tsl = slice(ci * L, (ci + 1) * L)
            for g in range(ng):
                u = d * ng + g
                rows = [d * nh + g * nj + j for j in range(nj)]
                x = x_ref[0, tsl, g * gw:(g + 1) * gw]
                bm = b_ref[0, tsl, g * ns:(g + 1) * ns]
                cm = c_ref[0, tsl, g * ns:(g + 1) * ns]
                cbm = lax.dot_general(cm, bm, _NT, preferred_element_type=F32)
                bmt = bm.T.astype(F32)
                yd = None
                st = None
                esc = []
                for i in range(nj // 2):
                    ws, bws, xs = [], [], []
                    for j in (2 * i, 2 * i + 1):
                        r = rows[j]
                        acs_c = jnp.broadcast_to(k_ref[0, ci, :, r:r + 1], (L, L))
                        dec = jnp.exp(jnp.where(tri, acs_c - r_ref[0, ci, nr + r:nr + r + 1, :], -jnp.inf))
                        ws.append((cbm * dec).astype(BF16))
                        bws.append((bmt * r_ref[0, ci, 2 * nr + r:2 * nr + r + 1, :]).astype(BF16))
                        xs.append(x * hmask[j])
                        esc.append(jnp.exp(acs_c))
                    xk = jnp.concatenate(xs, axis=0)
                    part = _dot(jnp.concatenate(ws, axis=1), xk)
                    spart = _dot(jnp.concatenate(bws, axis=1), xk)
                    yd = part if yd is None else yd + part
                    st = spart if st is None else st + spart
                hprev = hst_ref[u]
                yoff = _dot(cm, hprev.astype(BF16))
                y_ref[0, tsl, g * gw:(g + 1) * gw] = (yd + yoff * pair_lanes(esc)).astype(y_ref.dtype)
                dsc = pair_lanes([r_ref[0, ci, 3 * nr + r:3 * nr + r + 1, :] for r in rows])
                hst_ref[u] = dsc * hprev + st


def _ssd(xbc, rows, cols):
    b, s, _ = xbc.shape
    tl = CPS * CHUNK
    nblk = s // tl
    xw = SSD_GROUPS * SSD_HPG * SSD_P
    bw = SSD_GROUPS * SSD_N
    assert xw == 2 * bw and 2 * SSD_P == LANES

    def spec(width, j, rev):
        if rev:
            return pl.BlockSpec((1, tl, width), lambda i, c: (i, nblk - 1 - c, j))
        return pl.BlockSpec((1, tl, width), lambda i, c: (i, c, j))

    return pl.pallas_call(
        _ssd_kernel,
        grid=(b, nblk),
        in_specs=[spec(xw, 0, False), spec(bw, 2, False), spec(bw, 3, False),
                  spec(xw, 0, True), spec(bw, 2, True), spec(bw, 3, True),
                  *_chunk_specs((CHUNK, CHUNK), nblk), *_chunk_specs((CHUNK, CHUNK), nblk)],
        out_specs=[spec(xw, 0, False), spec(xw, 0, True)],
        out_shape=[jax.ShapeDtypeStruct((b, s, xw), BF16)] * 2,
        scratch_shapes=[pltpu.VMEM((2 * SSD_GROUPS, SSD_N, SSD_HPG * SSD_P), F32)],
        compiler_params=_cparams("parallel", "arbitrary"),
        name="ssd",
    )(xbc, xbc, xbc, xbc, xbc, xbc, rows, rows, cols, cols)


def _dft_consts(s):
    s1 = s // LANES
    c = np.arange(FNET_DIM)
    ang = 2.0 * np.pi * np.outer(c, c) / FNET_DIM
    dft = np.concatenate([np.cos(ang), -np.sin(ang)], axis=1)
    k1 = np.arange(s1)
    a1 = 2.0 * np.pi * np.outer(k1, k1) / s1
    c1, sn1 = np.cos(a1), np.sin(a1)
    w1 = np.block([[c1, sn1], [-sn1, c1]])
    s2 = np.arange(LANES)
    at = 2.0 * np.pi * (s2[:, None] * k1[None, :]) / s
    tc = np.broadcast_to(np.cos(at)[:, :, None], (LANES, s1, LANES))
    ts = np.broadcast_to(np.sin(at)[:, :, None], (LANES, s1, LANES))
    a2 = 2.0 * np.pi * np.outer(s2, s2) / LANES
    w2 = np.concatenate([np.cos(a2), np.sin(a2)], axis=1)
    scale = 1.0 / math.sqrt(s * FNET_DIM)
    return (jnp.asarray(dft, BF16), jnp.asarray(w1, BF16), jnp.asarray(tc, F32), jnp.asarray(ts, F32),
            jnp.asarray(w2, BF16), scale)


def _encode(x, p, consts):
    dft, w1, tc, ts, w2, fscale = consts
    v, qvo, kt, gt = _a_in(x, p["a_norm"], p["a_wf"], dft, p["a_wq"], p["a_wkt"], p["a_wgt"])
    fn = _fft2(_fft1(v, w1, tc, ts), w2, fscale)
    hf, hb = _mlstm(qvo, kt, *_mlstm_gates(gt, p["a_gbias"]))
    ffn = lambda i: [p["f_norm"][i], p["f_wu"][i], p["f_cw"][i], p["f_cb"][i], p["f_wd"][i], p["final"]]
    x = _a_tail(x, fn, hf, hb, qvo, p["a_wout"], ffn(0), False)
    z, xbc, yc, dtt = _c_in(x, p["c_norm"], p["c_w"], p["c_wdt"], p["c_scw"], p["c_scb"], p["c_ccw"], p["c_ccb"],
                            p["c_lng"], p["c_lnb"])
    yf, yb = _ssd(xbc, *_ssd_gates(dtt, p["c_dtbias"], p["c_aneg"]))
    return _c_tail(x, yf, yb, xbc, z, yc, p["c_dsk"], p["c_sng"], p["c_wout"], ffn(1), True)


def _prep(a_norm, a_w_in, a_gate_bias, a_w_out, c_norm, c_w_in, c_ssd_conv_w, c_ssd_conv_b, c_dt_bias, c_a_log,
          c_d_skip, c_ssd_norm, c_conf_conv_w, c_conf_conv_b, c_conf_ln_g, c_conf_ln_b, c_w_out,
          ffn_norm, ffn_w_up, ffn_conv_w, ffn_conv_b, ffn_w_down, final_norm):
    fw = FNET_GROUPS * FNET_DIM
    mw = MLSTM_HEADS * MLSTM_DIM
    zw = SSD_GROUPS * SSD_HPG * SSD_P
    xw = zw + 2 * SSD_GROUPS * SSD_N
    ndt = 2 * SSD_GROUPS * SSD_HPG
    wa = a_w_in[0]
    wc = c_w_in[0]
    row = lambda v: v.reshape(1, -1).astype(F32)
    return {
        "a_norm": row(a_norm[0]),
        "a_wf": wa[:, :fw].astype(BF16),
        "a_wq": jnp.concatenate([wa[:, fw:fw + mw], wa[:, fw + 2 * mw:fw + 4 * mw]], axis=1).astype(BF16),
        "a_wkt": wa[:, fw + mw:fw + 2 * mw].T.astype(BF16),
        "a_wgt": wa[:, fw + 4 * mw:].T.astype(BF16),
        "a_gbias": a_gate_bias[0].reshape(-1, 1).astype(F32),
        "a_wout": a_w_out[0].astype(BF16),
        "c_norm": row(c_norm[0]),
        "c_w": jnp.concatenate([wc[:, :zw + xw], wc[:, zw + xw + ndt:]], axis=1).astype(BF16),
        "c_wdt": wc[:, zw + xw:zw + xw + ndt].T.astype(BF16),
        "c_scw": c_ssd_conv_w[0].astype(F32),
        "c_scb": row(c_ssd_conv_b[0]),
        "c_dtbias": c_dt_bias[0].reshape(-1, 1).astype(F32),
        "c_aneg": (-jnp.exp(c_a_log[0].astype(F32))).reshape(-1, 1),
        "c_dsk": jnp.repeat(c_d_skip[0].reshape(-1).astype(F32), SSD_P).reshape(1, -1),
        "c_sng": row(c_ssd_norm[0]),
        "c_ccw": c_conf_conv_w[0].astype(F32),
        "c_ccb": row(c_conf_conv_b[0]),
        "c_lng": row(c_conf_ln_g[0]),
        "c_lnb": row(c_conf_ln_b[0]),
        "c_wout": c_w_out[0].astype(BF16),
        "f_norm": [row(ffn_norm[i]) for i in range(2)],
        "f_wu": [ffn_w_up[i].astype(BF16) for i in range(2)],
        "f_cw": [ffn_conv_w[i].astype(F32) for i in range(2)],
        "f_cb": [row(ffn_conv_b[i]) for i in range(2)],
        "f_wd": [ffn_w_down[i].astype(BF16) for i in range(2)],
        "final": row(final_norm),
    }


def kernel(x_prompt, x_sample, a_norm, a_w_in, a_gate_bias, a_w_out, c_norm, c_w_in, c_ssd_conv_w, c_ssd_conv_b, c_dt_bias, c_a_log, c_d_skip, c_ssd_norm, c_conf_conv_w, c_conf_conv_b, c_conf_ln_g, c_conf_ln_b, c_w_out, ffn_norm, ffn_w_up, ffn_conv_w, ffn_conv_b, ffn_w_down, final_norm):
    assert a_norm.shape[0] == 1 and c_norm.shape[0] == 1 and ffn_norm.shape[0] == 2
    p = _prep(a_norm, a_w_in, a_gate_bias, a_w_out, c_norm, c_w_in, c_ssd_conv_w, c_ssd_conv_b, c_dt_bias, c_a_log,
              c_d_skip, c_ssd_norm, c_conf_conv_w, c_conf_conv_b, c_conf_ln_g, c_conf_ln_b, c_w_out,
              ffn_norm, ffn_w_up, ffn_conv_w, ffn_conv_b, ffn_w_down, final_norm)
    outs = []
    consts = {}
    for x in (x_prompt, x_sample):
        s = x.shape[1]
        assert s % (LANES * SUBLANES) == 0
        if s not in consts:
            consts[s] = _dft_consts(s)
        outs.append(_encode(x.astype(F32), p, consts[s]))
    return tuple(outs)
```

```python
import functools
import math

import numpy as np
import jax
import jax.numpy as jnp
from jax import lax
from jax.experimental import pallas as pl
from jax.experimental.pallas import tpu as pltpu

F32 = jnp.float32
BF16 = jnp.bfloat16
EPS = 1e-6

LANES = 128
SUBLANES = 8
CHUNK = 128
VMEM_LIMIT = 56 * 1024 * 1024

FNET_GROUPS = 4
FNET_DIM = 128
MLSTM_HEADS = 4
MLSTM_DIM = 128
SSD_GROUPS = 4
SSD_HPG = 4
SSD_P = 64
SSD_N = 128
CONF_W = 512

TM_PROJ = 512
TM_FFN = 512
HALO_CONF = 16
HALO_T = 16
EXT = 8
CPS = 8
FFN_HC = 512

_NT = (((1,), (1,)), ((), ()))
_TN = (((0,), (0,)), ((), ()))


def _cparams(*sem):
    return pltpu.CompilerParams(dimension_semantics=sem, vmem_limit_bytes=VMEM_LIMIT)


def _const_spec(shape):
    nd = len(shape)
    return pl.BlockSpec(shape, lambda *_: (0,) * nd, pipeline_mode=pl.Buffered(1))


def _dot(a, b):
    return jnp.dot(a, b, preferred_element_type=F32)


def _rms(x, g):
    return x * lax.rsqrt(jnp.mean(x * x, axis=-1, keepdims=True) + EPS) * g


def _softplus(x):
    return jnp.maximum(x, 0.0) + jnp.log1p(jnp.exp(-jnp.abs(x)))


def _log_sigmoid(x):
    return -_softplus(-x)


def _silu(x):
    return x * jax.nn.sigmoid(x)


def _scan_lanes(x, combine, fill):
    ax = x.ndim - 1
    lane = lax.broadcasted_iota(jnp.int32, x.shape, ax)
    pre, suf = x, x
    k = 1
    while k < CHUNK:
        pre = combine(pre, jnp.where(lane >= k, pltpu.roll(pre, k, axis=ax), fill))
        suf = combine(suf, jnp.where(lane < CHUNK - k, pltpu.roll(suf, CHUNK - k, axis=ax), fill))
        k *= 2
    return pre, suf


def _cols_out(cols_ref, stk_scr):
    def body(c, carry):
        cols_ref[0, c] = stk_scr[c].T
        return carry
    lax.fori_loop(0, stk_scr.shape[0], body, 0, unroll=4)


def _pack2(hi, lo):
    hb = lax.bitcast_convert_type(hi.astype(BF16).astype(F32), jnp.uint32)
    lb = lax.bitcast_convert_type(lo.astype(BF16).astype(F32), jnp.uint32)
    return hb | (lb >> 16)


def _unpack2(w):
    hi = lax.bitcast_convert_type(w & jnp.uint32(0xFFFF0000), F32)
    lo = lax.bitcast_convert_type(w << 16, F32)
    return hi.astype(BF16), lo.astype(BF16)


def _chunk_specs(shape, nblk):
    fwd = pl.BlockSpec((1, CPS) + shape, lambda i, c: (i, c, 0, 0))
    bwd = pl.BlockSpec((1, CPS) + shape, lambda i, c: (i, nblk - 1 - c, 0, 0))
    return fwd, bwd


def _a_in_kernel(x_ref, g_ref, wf_ref, dft_ref, wq_ref, wkt_ref, wgt_ref, v_ref, qvo_ref, kt_ref, gt_ref):
    xb = _rms(x_ref[0], g_ref[...]).astype(BF16)
    hf = _dot(xb, wf_ref[...]).astype(BF16)
    for g in range(FNET_GROUPS):
        pq = _dot(hf[:, g * FNET_DIM:(g + 1) * FNET_DIM], dft_ref[...])
        v_ref[0, g] = _pack2(pq[:, :FNET_DIM], pq[:, FNET_DIM:])
    mw = MLSTM_HEADS * MLSTM_DIM
    qvo_ref[0, :, 0:mw] = (_dot(xb, wq_ref[:, 0:mw]) * (MLSTM_DIM ** -0.5)).astype(BF16)
    for j in range(1, qvo_ref.shape[2] // mw):
        sl = slice(j * mw, (j + 1) * mw)
        qvo_ref[0, :, sl] = _dot(xb, wq_ref[:, sl]).astype(BF16)
    kt_ref[0] = lax.dot_general(wkt_ref[...], xb, _NT, preferred_element_type=F32).astype(BF16)
    gt = lax.dot_general(wgt_ref[...], xb, _NT, preferred_element_type=F32)
    for c in range(gt_ref.shape[1]):
        gt_ref[0, c] = gt[:, c * CHUNK:(c + 1) * CHUNK]


def _a_in(x, g, wf, dft, wq, wkt, wgt):
    b, s, d = x.shape
    tm = min(TM_PROJ, s)
    fw = wf.shape[1]
    nq = wq.shape[1]
    kw = wkt.shape[0]
    ng = wgt.shape[0]
    return pl.pallas_call(
        _a_in_kernel,
        grid=(b, s // tm),
        in_specs=[
            pl.BlockSpec((1, tm, d), lambda i, m: (i, m, 0)),
            _const_spec((1, d)), _const_spec(wf.shape), _const_spec(dft.shape),
            _const_spec(wq.shape), _const_spec(wkt.shape), _const_spec(wgt.shape),
        ],
        out_specs=[
            pl.BlockSpec((1, FNET_GROUPS, tm, FNET_DIM), lambda i, m: (i, 0, m, 0)),
            pl.BlockSpec((1, tm, nq), lambda i, m: (i, m, 0)),
            pl.BlockSpec((1, kw, tm), lambda i, m: (i, 0, m)),
            pl.BlockSpec((1, tm // CHUNK, ng, CHUNK), lambda i, m: (i, m, 0, 0)),
        ],
        out_shape=[
            jax.ShapeDtypeStruct((b, FNET_GROUPS, s, FNET_DIM), jnp.uint32),
            jax.ShapeDtypeStruct((b, s, nq), BF16),
            jax.ShapeDtypeStruct((b, kw, s), BF16),
            jax.ShapeDtypeStruct((b, s // CHUNK, ng, CHUNK), F32),
        ],
        compiler_params=_cparams("parallel", "parallel"),
        name="a_in",
    )(x, g, wf, dft, wq, wkt, wgt)


def _fft1_kernel(v_ref, w1_ref, tc_ref, ts_ref, y_ref, vflat):
    ng, s1, ts2 = v_ref.shape[1:4]
    vflat[...] = v_ref[0].reshape(ng, s1 * ts2, LANES)
    for i in range(ts2):
        vr, vi = zip(*[_unpack2(vflat[g, pl.ds(i, s1, stride=ts2), :]) for g in range(ng)])
        v = jnp.concatenate([jnp.concatenate(vr, axis=1), jnp.concatenate(vi, axis=1)], axis=0)
        y = _dot(w1_ref[...], v)
        tc = tc_ref[i]
        ts = ts_ref[i]
        for g in range(ng):
            yr = y[:s1, g * LANES:(g + 1) * LANES]
            yi = y[s1:, g * LANES:(g + 1) * LANES]
            y_ref[0, g, i] = _pack2(yr * tc + yi * ts, yi * tc - yr * ts)


def _fft1(v, w1, tc, ts):
    b, ng, s, _ = v.shape
    s1 = s // LANES
    ts2 = SUBLANES
    v5 = v.reshape(b, ng, s1, LANES, LANES)
    spec_tw = pl.BlockSpec((ts2, s1, LANES), lambda j, i: (j, 0, 0))
    return pl.pallas_call(
        _fft1_kernel,
        grid=(LANES // ts2, b),
        in_specs=[pl.BlockSpec((1, ng, s1, ts2, LANES), lambda j, i: (i, 0, 0, j, 0)),
                  _const_spec(w1.shape), spec_tw, spec_tw],
        out_specs=pl.BlockSpec((1, ng, ts2, s1, LANES), lambda j, i: (i, 0, j, 0, 0)),
        out_shape=jax.ShapeDtypeStruct((b, ng, LANES, s1, LANES), jnp.uint32),
        scratch_shapes=[pltpu.VMEM((ng, s1 * ts2, LANES), jnp.uint32)],
        compiler_params=_cparams("parallel", "parallel"),
        name="fft1",
    )(v5, w1, tc, ts)


def _fft2_kernel(y_ref, w2_ref, o_ref, yflat, oflat, *, scale):
    ng, ns2, tk1 = y_ref.shape[1:4]
    yflat[...] = y_ref[0].reshape(ng, ns2 * tk1, LANES)
    for i in range(tk1):
        yr, yi = zip(*[_unpack2(yflat[g, pl.ds(i, ns2, stride=tk1), :]) for g in range(ng)])
        y = jnp.concatenate([jnp.concatenate(yr, axis=1), jnp.concatenate(yi, axis=1)], axis=0)
        out = _dot(w2_ref[...], y) * scale
        for g in range(ng):
            oflat[g, pl.ds(i, ns2, stride=tk1), :] = out[:, g * LANES:(g + 1) * LANES]
    o_ref[0] = oflat[...].reshape(ng, ns2, tk1, LANES)


def _fft2(y, w2, scale):
    b, ng, _, s1, _ = y.shape
    tk1 = SUBLANES
    spec = pl.BlockSpec((1, ng, LANES, tk1, LANES), lambda i, j: (i, 0, 0, j, 0))
    out = pl.pallas_call(
        functools.partial(_fft2_kernel, scale=scale),
        grid=(b, s1 // tk1),
        in_specs=[spec, _const_spec(w2.shape)],
        out_specs=spec,
        out_shape=jax.ShapeDtypeStruct((b, ng, LANES, s1, LANES), F32),
        scratch_shapes=[pltpu.VMEM((ng, LANES * tk1, LANES), jnp.uint32),
                        pltpu.VMEM((ng, LANES * tk1, LANES), F32)],
        compiler_params=_cparams("parallel", "parallel"),
        name="fft2",
    )(y, w2)
    return out.reshape(b, ng, LANES * s1, LANES)


def _mlstm_gates_kernel(gt_ref, bias_ref, rows_ref, cols_ref, tot_scr, amax_scr, mpf_scr, mpb_scr, stk_scr):
    nh = MLSTM_HEADS
    nu = 2 * nh
    nc = gt_ref.shape[1]
    g = gt_ref[0]
    bias = bias_ref[...][None]
    li = g[:, 0:nu] + bias[:, 0:nu]
    lf = _log_sigmoid(g[:, nu:] + bias[:, nu:])
    isf = lax.broadcasted_iota(jnp.int32, li.shape, 1) < nh
    pre, suf = _scan_lanes(lf, jnp.add, 0.0)
    bcs = jnp.where(isf, pre, suf)
    tot = pre + suf - lf
    e = li - bcs
    a = tot - bcs + li
    pa, sa = _scan_lanes(a, jnp.maximum, -jnp.inf)
    amax = jnp.maximum(pa, sa)
    pe, se = _scan_lanes(e, jnp.maximum, -jnp.inf)
    cmax = jnp.where(isf, pe, se)
    tot_scr[...] = tot
    amax_scr[...] = amax

    def fwd(c, m):
        mpf_scr[c] = m
        return jnp.maximum(tot_scr[c] + m, amax_scr[c])

    def bwd(i, m):
        c = nc - 1 - i
        mpb_scr[c] = m
        return jnp.maximum(tot_scr[c] + m, amax_scr[c])

    zero = jnp.zeros((nu, CHUNK), F32)
    lax.fori_loop(0, nc, fwd, zero)
    lax.fori_loop(0, nc, bwd, zero)
    m_prev = jnp.where(isf, mpf_scr[...], mpb_scr[...])
    m_new = jnp.maximum(tot + m_prev, amax)
    mu = jnp.maximum(m_prev, cmax)
    rows_ref[0, :, 0:nu] = e
    rows_ref[0, :, nu:2 * nu] = jnp.exp(a - m_new)
    rows_ref[0, :, 2 * nu:3 * nu] = jnp.exp(tot + m_prev - m_new)
    stk_scr[:, 0:nu] = mu
    stk_scr[:, nu:2 * nu] = jnp.exp(m_prev - mu)
    stk_scr[:, 2 * nu:3 * nu] = jnp.exp(-(bcs + mu))
    stk_scr[:, 3 * nu:] = jnp.zeros((nc, CHUNK - 3 * nu, CHUNK), F32)
    _cols_out(cols_ref, stk_scr)


def _mlstm_gates(gt, bias_col):
    b, nc, ng, _ = gt.shape
    nu = ng // 2
    small = pltpu.VMEM((nc, nu, CHUNK), F32)
    return pl.pallas_call(
        _mlstm_gates_kernel,
        grid=(b,),
        in_specs=[pl.BlockSpec((1, nc, ng, CHUNK), lambda i: (i, 0, 0, 0)), _const_spec(bias_col.shape)],
        out_specs=[pl.BlockSpec((1, nc, 3 * nu, CHUNK), lambda i: (i, 0, 0, 0)),
                   pl.BlockSpec((1, nc, CHUNK, CHUNK), lambda i: (i, 0, 0, 0))],
        out_shape=[jax.ShapeDtypeStruct((b, nc, 3 * nu, CHUNK), F32),
                   jax.ShapeDtypeStruct((b, nc, CHUNK, CHUNK), F32)],
        scratch_shapes=[small, small, small, small, pltpu.VMEM((nc, CHUNK, CHUNK), F32)],
        compiler_params=_cparams("parallel"),
        name="mlstm_gates",
    )(gt, bias_col)


def _mlstm_kernel(qf_ref, ktf_ref, vf_ref, qb_ref, ktb_ref, vb_ref, rf_ref, rb_ref, cf_ref, cb_ref,
                  hf_ref, hb_ref, cst_ref):
    nh, dh = MLSTM_HEADS, MLSTM_DIM
    nu = 2 * nh
    L = CHUNK

    @pl.when(pl.program_id(1) == 0)
    def _():
        cst_ref[...] = jnp.zeros_like(cst_ref)

    t_i = lax.broadcasted_iota(jnp.int32, (L, L), 0)
    s_i = lax.broadcasted_iota(jnp.int32, (L, L), 1)
    top = lax.broadcasted_iota(jnp.int32, (2 * dh, L), 0) < dh
    ones = jnp.ones((L, dh), BF16)
    for d in range(2):
        q_ref, kt_ref, v_ref, r_ref, c_ref, o_ref = ((qf_ref, ktf_ref, vf_ref, rf_ref, cf_ref, hf_ref) if d == 0 else
                                                     (qb_ref, ktb_ref, vb_ref, rb_ref, cb_ref, hb_ref))
        tri = (s_i <= t_i) if d == 0 else (s_i >= t_i)
        for step in range(CPS):
            ci = step if d == 0 else CPS - 1 - step
            tsl = slice(ci * L, (ci + 1) * L)
            for hp in range(nh // 2):
                psl = slice(2 * hp * dh, (2 * hp + 2) * dh)
                kt2 = kt_ref[0, psl, tsl]
                zero = jnp.zeros_like(kt2)
                kbd = jnp.concatenate([jnp.where(top, kt2, zero), jnp.where(top, zero, kt2)], axis=1)
                sc2 = _dot(q_ref[0, tsl, psl], kbd)
                for hh in range(2):
                    h = 2 * hp + hh
                    u = d * nh + h
                    sl = slice(h * dh, (h + 1) * dh)
                    q = q_ref[0, tsl, sl]
                    v = v_ref[0, tsl, sl]
                    vaug = jnp.concatenate([v, ones], axis=1)
                    e_r = r_ref[0, ci, u:u + 1, :]
                    ea_r = r_ref[0, ci, nu + u:nu + u + 1, :]
                    decay = r_ref[0, ci, 2 * nu + u:2 * nu + u + 1, 0:1]
                    mu_c = c_ref[0, ci, :, u:u + 1]
                    isc_c = c_ref[0, ci, :, nu + u:nu + u + 1]
                    emt_c = c_ref[0, ci, :, 2 * nu + u:2 * nu + u + 1]
                    w = jnp.exp(jnp.where(tri, e_r - mu_c, -jnp.inf))
                    sw = (sc2[:, hh * L:(hh + 1) * L] * w).astype(BF16)
                    qi = (q.astype(F32) * isc_c).astype(BF16)
                    cs = cst_ref[u]
                    tot = _dot(jnp.concatenate([sw, qi], axis=1),
                               jnp.concatenate([vaug, cs.astype(BF16)], axis=0))
                    den = jnp.maximum(jnp.abs(tot[:, dh:]), emt_c)
                    o_ref[0, tsl, sl] = (tot[:, :dh] / den).astype(o_ref.dtype)
                    kw = (kt_ref[0, sl, tsl].astype(F32) * ea_r).astype(BF16)
                    cst_ref[u] = decay * cs + _dot(kw, vaug)


def _mlstm(qvo, kt, rows, cols):
    b, s, _ = qvo.shape
    tl = CPS * CHUNK
    nblk = s // tl
    w = MLSTM_HEADS * MLSTM_DIM
    nr = rows.shape[2]

    def tok(j, rev):
        if rev:
            return pl.BlockSpec((1, tl, w), lambda i, c: (i, nblk - 1 - c, j))
        return pl.BlockSpec((1, tl, w), lambda i, c: (i, c, j))

    ktf = pl.BlockSpec((1, w, tl), lambda i, c: (i, 0, c))
    ktb = pl.BlockSpec((1, w, tl), lambda i, c: (i, 0, nblk - 1 - c))
    return pl.pallas_call(
        _mlstm_kernel,
        grid=(b, nblk),
        in_specs=[tok(0, False), ktf, tok(1, False), tok(0, True), ktb, tok(1, True),
                  *_chunk_specs((nr, CHUNK), nblk), *_chunk_specs((CHUNK, CHUNK), nblk)],
        out_specs=[tok(0, False), tok(0, True)],
        out_shape=[jax.ShapeDtypeStruct((b, s, w), BF16)] * 2,
        scratch_shapes=[pltpu.VMEM((2 * MLSTM_HEADS, MLSTM_DIM, 2 * MLSTM_DIM), F32)],
        compiler_params=_cparams("parallel", "arbitrary"),
        name="mlstm",
    )(qvo, kt, qvo, qvo, kt, qvo, rows, rows, cols, cols)


def _halo_specs(tm, halo, width, s, col=0, groups=None):
    r = tm // halo
    nb = s // halo
    rows = [(tm, lambda m: m), (halo, lambda m: jnp.maximum(m * r - 1, 0)),
            (halo, lambda m: jnp.minimum((m + 1) * r, nb - 1))]
    if groups is None:
        return [pl.BlockSpec((1, n, width), lambda i, m, f=f: (i, f(m), col)) for n, f in rows]
    return [pl.BlockSpec((1, groups, n, width), lambda i, m, f=f: (i, 0, f(m), 0)) for n, f in rows]


def _ext(refs, g=None):
    idx = (0,) if g is None else (0, g)
    mid, top, bot = (r[idx].astype(F32) for r in refs)
    return jnp.concatenate([top[HALO_T - EXT:], mid, bot[:EXT]], axis=0)


def _ffn_body(xa, g_ref, wu_ref, cw_ref, cb_ref, wd_ref, fg_ref, y_ref, a_scr, final_norm):
    m = pl.program_id(1)
    tm = y_ref.shape[1]
    fh = wd_ref.shape[0]
    n_ext = xa.shape[0]
    row = lax.broadcasted_iota(jnp.int32, (n_ext, 1), 0)
    first = jnp.where(m > 0, 0, EXT)
    last = jnp.where(m < pl.num_programs(1) - 1, n_ext, EXT + tm)
    xn = _rms(xa, g_ref[...])
    xe = jnp.where(row >= first, jnp.where(row < last, xn, 0.0), 0.0).astype(BF16)

    def conv3(h, c0):
        w = cw_ref[:, c0:c0 + FFN_HC]
        out = (w[0:1] * pltpu.roll(h, 1, axis=0) + w[1:2] * h + w[2:3] * pltpu.roll(h, n_ext - 1, axis=0))
        return out[EXT:EXT + tm] + cb_ref[:, c0:c0 + FFN_HC]

    for j in range(fh // FFN_HC):
        c0 = j * FFN_HC
        gate = conv3(_dot(xe, wu_ref[:, c0:c0 + FFN_HC]), c0)
        val = conv3(_dot(xe, wu_ref[:, fh + c0:fh + c0 + FFN_HC]), fh + c0)
        a_scr[:, c0:c0 + FFN_HC] = (_silu(gate) * val).astype(BF16)
    y = xa[EXT:EXT + tm] + _dot(a_scr[...], wd_ref[...])
    if final_norm:
        y = _rms(y, fg_ref[...])
    y_ref[0] = y


def _a_tail_kernel(*refs, final_norm):
    x, fn, hf, hb, o = (refs[3 * i:3 * i + 3] for i in range(5))
    wo_ref, g_ref, wu_ref, cw_ref, cb_ref, wd_ref, fg_ref, y_ref, a_scr = refs[15:]
    ng = fn[0].shape[1]
    mix = [_ext(fn, g) for g in range(ng)] + [jax.nn.sigmoid(_ext(o)) * (_ext(hf) + _ext(hb))]
    xa = _ext(x) + _dot(jnp.concatenate(mix, axis=1).astype(BF16), wo_ref[...])
    _ffn_body(xa, g_ref, wu_ref, cw_ref, cb_ref, wd_ref, fg_ref, y_ref, a_scr, final_norm)


def _c_tail_kernel(*refs, final_norm):
    x, yf, yb, xs, z, yc = (refs[3 * i:3 * i + 3] for i in range(6))
    dsk_ref, sng_ref, wo_ref, g_ref, wu_ref, cw_ref, cb_ref, wd_ref, fg_ref, y_ref, a_scr = refs[18:]
    y = (_ext(yf) + _ext(yb) + dsk_ref[...] * _ext(xs)) * _silu(_ext(z))
    mix = jnp.concatenate([_rms(y, sng_ref[...]), _ext(yc)], axis=1).astype(BF16)
    xa = _ext(x) + _dot(mix, wo_ref[...])
    _ffn_body(xa, g_ref, wu_ref, cw_ref, cb_ref, wd_ref, fg_ref, y_ref, a_scr, final_norm)


def _tail_call(kernel, name, x, tok_specs, tok_args, consts, wd, final_norm):
    b, s, d = x.shape
    tm = min(TM_FFN, s)
    return pl.pallas_call(
        functools.partial(kernel, final_norm=final_norm),
        grid=(b, s // tm),
        in_specs=tok_specs + [_const_spec(a.shape) for a in consts],
        out_specs=pl.BlockSpec((1, tm, d), lambda i, m: (i, m, 0)),
        out_shape=jax.ShapeDtypeStruct((b, s, d), F32),
        scratch_shapes=[pltpu.VMEM((tm, wd.shape[0]), BF16)],
        compiler_params=_cparams("parallel", "parallel"),
        name=name,
    )(*tok_args, *consts)


def _a_tail(x, fn, hf, hb, qvo, wo, ffn, final_norm):
    b, s, d = x.shape
    tm = min(TM_FFN, s)
    mw = hf.shape[2]
    hs = functools.partial(_halo_specs, tm, HALO_T)
    specs = (hs(d, s) + hs(fn.shape[3], s, groups=fn.shape[1]) + hs(mw, s) + hs(mw, s) + hs(mw, s, col=2))
    args = [a for a in (x, fn, hf, hb, qvo) for _ in range(3)]
    return _tail_call(_a_tail_kernel, "a_tail", x, specs, args, [wo] + ffn, ffn[4], final_norm)


def _c_tail(x, yf, yb, xbc, z, yc, dsk, sng, wo, ffn, final_norm):
    b, s, d = x.shape
    tm = min(TM_FFN, s)
    sw = yf.shape[2]
    hs = functools.partial(_halo_specs, tm, HALO_T)
    specs = hs(d, s) + hs(sw, s) + hs(sw, s) + hs(sw, s) + hs(sw, s) + hs(yc.shape[2], s)
    args = [a for a in (x, yf, yb, xbc, z, yc) for _ in range(3)]
    return _tail_call(_c_tail_kernel, "c_tail", x, specs, args, [dsk, sng, wo] + ffn, ffn[4], final_norm)


def _dwconv_taps(xe, w, tm, halo):
    k = w.shape[0]
    pad = (k - 1) // 2
    n = xe.shape[0]
    acc = None
    for j in range(k):
        off = halo - pad + j
        tap = w[j:j + 1] * pltpu.roll(xe, (n - off) % n, axis=0)[0:tm]
        acc = tap if acc is None else acc + tap
    return acc


def _c_in_kernel(x_ref, top_ref, bot_ref, g_ref, w_ref, wdt_ref, scw_ref, scb_ref, ccw_ref, ccb_ref, lg_ref, lb_ref,
                 z_ref, xbc_ref, yc_ref, dtt_ref):
    m = pl.program_id(1)
    tm = x_ref.shape[1]
    g = g_ref[...]
    zw = z_ref.shape[2]
    xw = xbc_ref.shape[2]
    cw = yc_ref.shape[2]
    ct = 512
    xm = _rms(x_ref[0], g)
    top = jnp.where(m > 0, _rms(top_ref[0], g), 0.0)
    bot = jnp.where(m < pl.num_programs(1) - 1, _rms(bot_ref[0], g), 0.0)
    xb = xm.astype(BF16)
    xe = jnp.concatenate([top, xm, bot], axis=0).astype(BF16)
    for j in range(zw // ct):
        sl = slice(j * ct, (j + 1) * ct)
        z_ref[0, :, sl] = _dot(xb, w_ref[:, sl]).astype(BF16)
    dtt = lax.dot_general(wdt_ref[...], xb, _NT, preferred_element_type=F32)
    for c in range(dtt_ref.shape[1]):
        dtt_ref[0, c] = dtt[:, c * CHUNK:(c + 1) * CHUNK]
    for j in range(xw // ct):
        sl = slice(j * ct, (j + 1) * ct)
        h = _dot(xe, w_ref[:, zw + j * ct:zw + (j + 1) * ct])
        xbc_ref[0, :, sl] = _silu(_dwconv_taps(h, scw_ref[:, sl], tm, HALO_CONF) + scb_ref[:, sl]).astype(BF16)
    val = _dot(xe, w_ref[:, zw + xw:zw + xw + cw])
    gate = _dot(xe, w_ref[:, zw + xw + cw:zw + xw + 2 * cw])
    u = _dwconv_taps(val * jax.nn.sigmoid(gate), ccw_ref[...], tm, HALO_CONF) + ccb_ref[...]
    mu = jnp.mean(u, axis=-1, keepdims=True)
    uc = u - mu
    var = jnp.mean(uc * uc, axis=-1, keepdims=True)
    yc_ref[0] = _silu(uc * lax.rsqrt(var + EPS) * lg_ref[...] + lb_ref[...]).astype(BF16)


def _c_in(x, g, w, wdt, scw, scb, ccw, ccb, lg, lb):
    b, s, d = x.shape
    tm = min(TM_PROJ, s)
    nd = wdt.shape[0]
    xw = scw.shape[1]
    cw = ccw.shape[1]
    zw = w.shape[1] - xw - 2 * cw
    mid, top, bot = _halo_specs(tm, HALO_CONF, d, s)
    tok = lambda width: pl.BlockSpec((1, tm, width), lambda i, m: (i, m, 0))
    consts = [g, w, wdt, scw, scb, ccw, ccb, lg, lb]
    return pl.pallas_call(
        _c_in_kernel,
        grid=(b, s // tm),
        in_specs=[mid, top, bot] + [_const_spec(a.shape) for a in consts],
        out_specs=[tok(zw), tok(xw), tok(cw),
                   pl.BlockSpec((1, tm // CHUNK, nd, CHUNK), lambda i, m: (i, m, 0, 0))],
        out_shape=[jax.ShapeDtypeStruct((b, s, zw), BF16), jax.ShapeDtypeStruct((b, s, xw), BF16),
                   jax.ShapeDtypeStruct((b, s, cw), BF16), jax.ShapeDtypeStruct((b, s // CHUNK, nd, CHUNK), F32)],
        compiler_params=_cparams("parallel", "parallel"),
        name="c_in",
    )(x, x, x, *consts)


def _ssd_gates_kernel(dt_ref, bias_ref, aneg_ref, rows_ref, cols_ref, stk_scr):
    nh = SSD_GROUPS * SSD_HPG
    nr = 2 * nh
    nc = dt_ref.shape[1]
    dt = _softplus(dt_ref[0] + bias_ref[...][None])
    da = dt * aneg_ref[...][None]
    isf = lax.broadcasted_iota(jnp.int32, dt.shape, 1) < nh
    pre, suf = _scan_lanes(da, jnp.add, 0.0)
    acs = jnp.where(isf, pre, suf)
    tot = pre + suf - da
    rows_ref[0, :, 0:nr] = dt
    rows_ref[0, :, nr:2 * nr] = acs - jnp.log(dt)
    rows_ref[0, :, 2 * nr:3 * nr] = jnp.exp(tot - acs) * dt
    rows_ref[0, :, 3 * nr:4 * nr] = jnp.exp(tot)
    stk_scr[:, 0:nr] = acs
    stk_scr[:, nr:] = jnp.zeros((nc, CHUNK - nr, CHUNK), F32)
    _cols_out(cols_ref, stk_scr)


def _ssd_gates(dtt, dtbias_col, aneg_col):
    b, nc, nr, _ = dtt.shape
    assert 4 * nr == CHUNK
    full = pl.BlockSpec((1, nc, CHUNK, CHUNK), lambda i: (i, 0, 0, 0))
    return pl.pallas_call(
        _ssd_gates_kernel,
        grid=(b,),
        in_specs=[pl.BlockSpec((1, nc, nr, CHUNK), lambda i: (i, 0, 0, 0)), _const_spec(dtbias_col.shape),
                  _const_spec(aneg_col.shape)],
        out_specs=[full, full],
        out_shape=[jax.ShapeDtypeStruct((b, nc, CHUNK, CHUNK), F32)] * 2,
        scratch_shapes=[pltpu.VMEM((nc, CHUNK, CHUNK), F32)],
        compiler_params=_cparams("parallel"),
        name="ssd_gates",
    )(dtt, dtbias_col, aneg_col)


def _ssd_kernel(xf_ref, bf_ref, cf_ref, xb_ref, bb_ref, cb_ref, rf_ref, rb_ref, kf_ref, kb_ref,
                yf_ref, yb_ref, hst_ref):
    ng, nj, hp, ns = SSD_GROUPS, SSD_HPG, SSD_P, SSD_N
    nh = ng * nj
    nr = 2 * nh
    gw = nj * hp
    L = CHUNK

    @pl.when(pl.program_id(1) == 0)
    def _():
        hst_ref[...] = jnp.zeros_like(hst_ref)

    t_i = lax.broadcasted_iota(jnp.int32, (L, L), 0)
    s_i = lax.broadcasted_iota(jnp.int32, (L, L), 1)
    hid = lax.broadcasted_iota(jnp.int32, (1, gw), 1) // hp
    hmask = [(hid == j).astype(F32).astype(BF16) for j in range(nj)]
    low = lax.broadcasted_iota(jnp.int32, (1, 2 * hp), 1) < hp

    def pair_lanes(vals):
        return jnp.concatenate([jnp.where(low, vals[2 * i], vals[2 * i + 1]) for i in range(nj // 2)], axis=1)

    for d in range(2):
        x_ref, b_ref, c_ref, r_ref, k_ref, y_ref = ((xf_ref, bf_ref, cf_ref, rf_ref, kf_ref, yf_ref) if d == 0 else
                                                    (xb_ref, bb_ref, cb_ref, rb_ref, kb_ref, yb_ref))
        tri = (s_i <= t_i) if d == 0 else (s_i >= t_i)
        for step in range(CPS):
            ci = step if d == 0 else CPS - 1 - step
            tsl = slice(ci * L, (ci + 1) * L)
            for g in range(ng):
                u = d * ng + g
                rows = [d * nh + g * nj + j for j in range(nj)]
                x = x_ref[0, tsl, g * gw:(g + 1) * gw]
                bm = b_ref[0, tsl, g * ns:(g + 1) * ns]
                cm = c_ref[0, tsl, g * ns:(g + 1) * ns]
                cbm = lax.dot_general(cm, bm, _NT, preferred_element_type=F32)
                bmt = bm.T.astype(F32)
                yd = None
                st = None
                esc = []
                for i in range(nj // 2):
                    ws, bws, xs = [], [], []
                    for j in (2 * i, 2 * i + 1):
                        r = rows[j]
                        acs_c = jnp.broadcast_to(k_ref[0, ci, :, r:r + 1], (L, L))
                        dec = jnp.exp(jnp.where(tri, acs_c - r_ref[0, ci, nr + r:nr + r + 1, :], -jnp.inf))
                        ws.append((cbm * dec).astype(BF16))
                        bws.append((bmt * r_ref[0, ci, 2 * nr + r:2 * nr + r + 1, :]).astype(BF16))
                        xs.append(x * hmask[j])
                        esc.append(jnp.exp(acs_c))
                    xk = jnp.concatenate(xs, axis=0)
                    part = _dot(jnp.concatenate(ws, axis=1), xk)
                    spart = _dot(jnp.concatenate(bws, axis=1), xk)
                    yd = part if yd is None else yd + part
                    st = spart if st is None else st + spart
                hprev = hst_ref[u]
                yoff = _dot(cm, hprev.astype(BF16))
                y_ref[0, tsl, g * gw:(g + 1) * gw] = (yd + yoff * pair_lanes(esc)).astype(y_ref.dtype)
                dsc = pair_lanes([r_ref[0, ci, 3 * nr + r:3 * nr + r + 1, :] for r in rows])
                hst_ref[u] = dsc * hprev + st


def _ssd(xbc, rows, cols):
    b, s, _ = xbc.shape
    tl = CPS * CHUNK
    nblk = s // tl
    xw = SSD_GROUPS * SSD_HPG * SSD_P
    bw = SSD_GROUPS * SSD_N
    assert xw == 2 * bw and 2 * SSD_P == LANES

    def spec(width, j, rev):
        if rev:
            return pl.BlockSpec((1, tl, width), lambda i, c: (i, nblk - 1 - c, j))
        return pl.BlockSpec((1, tl, width), lambda i, c: (i, c, j))

    return pl.pallas_call(
        _ssd_kernel,
        grid=(b, nblk),
        in_specs=[spec(xw, 0, False), spec(bw, 2, False), spec(bw, 3, False),
                  spec(xw, 0, True), spec(bw, 2, True), spec(bw, 3, True),
                  *_chunk_specs((CHUNK, CHUNK), nblk), *_chunk_specs((CHUNK, CHUNK), nblk)],
        out_specs=[spec(xw, 0, False), spec(xw, 0, True)],
        out_shape=[jax.ShapeDtypeStruct((b, s, xw), BF16)] * 2,
        scratch_shapes=[pltpu.VMEM((2 * SSD_GROUPS, SSD_N, SSD_HPG * SSD_P), F32)],
        compiler_params=_cparams("parallel", "arbitrary"),
        name="ssd",
    )(xbc, xbc, xbc, xbc, xbc, xbc, rows, rows, cols, cols)


def _dft_consts(s):
    s1 = s // LANES
    c = np.arange(FNET_DIM)
    ang = 2.0 * np.pi * np.outer(c, c) / FNET_DIM
    dft = np.concatenate([np.cos(ang), -np.sin(ang)], axis=1)
    k1 = np.arange(s1)
    a1 = 2.0 * np.pi * np.outer(k1, k1) / s1
    c1, sn1 = np.cos(a1), np.sin(a1)
    w1 = np.block([[c1, sn1], [-sn1, c1]])
    s2 = np.arange(LANES)
    at = 2.0 * np.pi * (s2[:, None] * k1[None, :]) / s
    tc = np.broadcast_to(np.cos(at)[:, :, None], (LANES, s1, LANES))
    ts = np.broadcast_to(np.sin(at)[:, :, None], (LANES, s1, LANES))
    a2 = 2.0 * np.pi * np.outer(s2, s2) / LANES
    w2 = np.concatenate([np.cos(a2), np.sin(a2)], axis=1)
    scale = 1.0 / math.sqrt(s * FNET_DIM)
    return (jnp.asarray(dft, BF16), jnp.asarray(w1, BF16), jnp.asarray(tc, F32), jnp.asarray(ts, F32),
            jnp.asarray(w2, BF16), scale)


def _encode(x, p, consts):
    dft, w1, tc, ts, w2, fscale = consts
    v, qvo, kt, gt = _a_in(x, p["a_norm"], p["a_wf"], dft, p["a_wq"], p["a_wkt"], p["a_wgt"])
    fn = _fft2(_fft1(v, w1, tc, ts), w2, fscale)
    hf, hb = _mlstm(qvo, kt, *_mlstm_gates(gt, p["a_gbias"]))
    ffn = lambda i: [p["f_norm"][i], p["f_wu"][i], p["f_cw"][i], p["f_cb"][i], p["f_wd"][i], p["final"]]
    x = _a_tail(x, fn, hf, hb, qvo, p["a_wout"], ffn(0), False)
    z, xbc, yc, dtt = _c_in(x, p["c_norm"], p["c_w"], p["c_wdt"], p["c_scw"], p["c_scb"], p["c_ccw"], p["c_ccb"],
                            p["c_lng"], p["c_lnb"])
    yf, yb = _ssd(xbc, *_ssd_gates(dtt, p["c_dtbias"], p["c_aneg"]))
    return _c_tail(x, yf, yb, xbc, z, yc, p["c_dsk"], p["c_sng"], p["c_wout"], ffn(1), True)


def _prep(a_norm, a_w_in, a_gate_bias, a_w_out, c_norm, c_w_in, c_ssd_conv_w, c_ssd_conv_b, c_dt_bias, c_a_log,
          c_d_skip, c_ssd_norm, c_conf_conv_w, c_conf_conv_b, c_conf_ln_g, c_conf_ln_b, c_w_out,
          ffn_norm, ffn_w_up, ffn_conv_w, ffn_conv_b, ffn_w_down, final_norm):
    fw = FNET_GROUPS * FNET_DIM
    mw = MLSTM_HEADS * MLSTM_DIM
    zw = SSD_GROUPS * SSD_HPG * SSD_P
    xw = zw + 2 * SSD_GROUPS * SSD_N
    ndt = 2 * SSD_GROUPS * SSD_HPG
    wa = a_w_in[0]
    wc = c_w_in[0]
    row = lambda v: v.reshape(1, -1).astype(F32)
    return {
        "a_norm": row(a_norm[0]),
        "a_wf": wa[:, :fw].astype(BF16),
        "a_wq": jnp.concatenate([wa[:, fw:fw + mw], wa[:, fw + 2 * mw:fw + 4 * mw]], axis=1).astype(BF16),
        "a_wkt": wa[:, fw + mw:fw + 2 * mw].T.astype(BF16),
        "a_wgt": wa[:, fw + 4 * mw:].T.astype(BF16),
        "a_gbias": a_gate_bias[0].reshape(-1, 1).astype(F32),
        "a_wout": a_w_out[0].astype(BF16),
        "c_norm": row(c_norm[0]),
        "c_w": jnp.concatenate([wc[:, :zw + xw], wc[:, zw + xw + ndt:]], axis=1).astype(BF16),
        "c_wdt": wc[:, zw + xw:zw + xw + ndt].T.astype(BF16),
        "c_scw": c_ssd_conv_w[0].astype(F32),
        "c_scb": row(c_ssd_conv_b[0]),
        "c_dtbias": c_dt_bias[0].reshape(-1, 1).astype(F32),
        "c_aneg": (-jnp.exp(c_a_log[0].astype(F32))).reshape(-1, 1),
        "c_dsk": jnp.repeat(c_d_skip[0].reshape(-1).astype(F32), SSD_P).reshape(1, -1),
        "c_sng": row(c_ssd_norm[0]),
        "c_ccw": c_conf_conv_w[0].astype(F32),
        "c_ccb": row(c_conf_conv_b[0]),
        "c_lng": row(c_conf_ln_g[0]),
        "c_lnb": row(c_conf_ln_b[0]),
        "c_wout": c_w_out[0].astype(BF16),
        "f_norm": [row(ffn_norm[i]) for i in range(2)],
        "f_wu": [ffn_w_up[i].astype(BF16) for i in range(2)],
        "f_cw": [ffn_conv_w[i].astype(F32) for i in range(2)],
        "f_cb": [row(ffn_conv_b[i]) for i in range(2)],
        "f_wd": [ffn_w_down[i].astype(BF16) for i in range(2)],
        "final": row(final_norm),
    }


def kernel(x_prompt, x_sample, a_norm, a_w_in, a_gate_bias, a_w_out, c_norm, c_w_in, c_ssd_conv_w, c_ssd_conv_b, c_dt_bias, c_a_log, c_d_skip, c_ssd_norm, c_conf_conv_w, c_conf_conv_b, c_conf_ln_g, c_conf_ln_b, c_w_out, ffn_norm, ffn_w_up, ffn_conv_w, ffn_conv_b, ffn_w_down, final_norm):
    assert a_norm.shape[0] == 1 and c_norm.shape[0] == 1 and ffn_norm.shape[0] == 2
    p = _prep(a_norm, a_w_in, a_gate_bias, a_w_out, c_norm, c_w_in, c_ssd_conv_w, c_ssd_conv_b, c_dt_bias, c_a_log,
              c_d_skip, c_ssd_norm, c_conf_conv_w, c_conf_conv_b, c_conf_ln_g, c_conf_ln_b, c_w_out,
              ffn_norm, ffn_w_up, ffn_conv_w, ffn_conv_b, ffn_w_down, final_norm)
    outs = []
    consts = {}
    for x in (x_prompt, x_sample):
        s = x.shape[1]
        assert s % (LANES * SUBLANES) == 0
        if s not in consts:
            consts[s] = _dft_consts(s)
        outs.append(_encode(x.astype(F32), p, consts[s]))
    return tuple(outs)
```
